```python
import math
import jax, jax.numpy as jnp
from jax import lax
import numpy as np

D_MODEL = 1024
BATCH = 8
SEQ = 2048
DEPTH = 4
DEC_BATCH = 128
DEC_SEQ = 4
PAST_LEN = 16384
PAGE_SIZE = 128

N_META = 16
N_MIXERS = 2
N_GDN = (DEPTH + 1) // 2
N_RWKV = DEPTH // 2
GDN_HEADS = 8
GDN_DK = 128
GDN_DV = 128
GDN_QK = GDN_HEADS * GDN_DK
GDN_V = GDN_HEADS * GDN_DV
GDN_CONV = 4
GDN_CONV_DIM = 2 * GDN_QK + GDN_V
GDN_IN = GDN_CONV_DIM + GDN_V + 2 * GDN_HEADS
GDN_CHUNK = 64
RWKV_N = 64
RWKV_HEADS = D_MODEL // RWKV_N
DECAY_LORA = 64
AAA_LORA = 64
MV_LORA = 32
GATE_LORA = 160
D_FF = 2816
FFN_CONV = 3
RMS_EPS = 1e-6
L2_EPS = 1e-6
GN_EPS = 64e-5

kernel_name = 'hybrid_gdn_rwkv7_convffn_meta_step'

F32 = jnp.float32


def rmsnorm(x, g):
    xf = x.astype(F32)
    y = xf * lax.rsqrt(jnp.mean(xf * xf, axis=-1, keepdims=True) + RMS_EPS)
    return (y * g.astype(F32)).astype(x.dtype)


def l2norm(x):
    xf = x.astype(F32)
    return xf * lax.rsqrt(jnp.sum(xf * xf, axis=-1, keepdims=True) + L2_EPS)


def causal_dwconv(buf, x, w):
    width = w.shape[0]
    L = x.shape[1]
    xc = jnp.concatenate([buf.astype(x.dtype), x], axis=1)
    out = xc[:, 0:L] * w[0]
    for j in range(1, width):
        out = out + xc[:, j:j + L] * w[j]
    return out, xc[:, L:]


def gdn_segment(q, k, v, g, beta, S):
    B, L, H, _ = q.shape
    c = min(GDN_CHUNK, L)
    n = -(-L // c)
    pad = n * c - L

    def blocks(t):
        t = jnp.pad(t, [(0, 0), (0, pad)] + [(0, 0)] * (t.ndim - 2))
        t = t.reshape((B, n, c) + t.shape[2:])
        return jnp.moveaxis(t, (1, 3), (0, 2))

    q, k, v, g, beta = blocks(q), blocks(k), blocks(v), blocks(g), blocks(beta)
    G = jnp.cumsum(g, axis=-1)
    idx = jnp.arange(c)
    causal = idx[:, None] >= idx[None, :]
    strict = idx[:, None] > idx[None, :]
    decay_mat = jnp.exp(jnp.where(causal, G[..., :, None] - G[..., None, :], -jnp.inf))
    kb = k * beta[..., None]
    kk = jnp.einsum('nbhid,nbhjd->nbhij', kb, k) * decay_mat
    M = jnp.eye(c, dtype=F32) + jnp.where(strict, kk, 0.0)
    rhs = jnp.concatenate([v * beta[..., None], kb * jnp.exp(G)[..., None]], axis=-1)
    sol = lax.linalg.triangular_solve(M, rhs, left_side=True, lower=True, unit_diagonal=True)
    u, w = sol[..., :GDN_DV], sol[..., GDN_DV:]
    qk = jnp.einsum('nbhid,nbhjd->nbhij', q, k) * decay_mat

    def step(S, xs):
        q_c, k_c, u_c, w_c, G_c, qk_c = xs
        v_new = u_c - jnp.einsum('bhik,bhkv->bhiv', w_c, S)
        o = (jnp.einsum('bhik,bhkv->bhiv', q_c * jnp.exp(G_c)[..., None], S)
             + jnp.einsum('bhij,bhjv->bhiv', qk_c, v_new))
        g_last = G_c[..., -1]
        k_dec = k_c * jnp.exp(g_last[..., None] - G_c)[..., None]
        S = S * jnp.exp(g_last)[..., None, None] + jnp.einsum('bhik,bhiv->bhkv', k_dec, v_new)
        return S, o

    S, o = lax.scan(step, S, (q, k, u, w, G, qk))
    o = jnp.transpose(o, (1, 0, 3, 2, 4)).reshape(B, n * c, H, GDN_DV)[:, :L]
    return o, S


def gdn_mixer(h, conv_buf, S0, w_in, conv_w, A_log, dt_bias, norm_w, w_out, segments):
    B, L, _ = h.shape
    proj = jnp.einsum('bld,de->ble', h, w_in)
    qkv = proj[..., :GDN_CONV_DIM]
    z = proj[..., GDN_CONV_DIM:GDN_CONV_DIM + GDN_V]
    a = proj[..., GDN_CONV_DIM + GDN_V:GDN_CONV_DIM + GDN_V + GDN_HEADS].astype(F32)
    b = proj[..., GDN_CONV_DIM + GDN_V + GDN_HEADS:].astype(F32)
    qkv_c, new_buf = causal_dwconv(conv_buf, qkv, conv_w)
    qkv_c = jax.nn.silu(qkv_c)
    q = l2norm(qkv_c[..., :GDN_QK].reshape(B, L, GDN_HEADS, GDN_DK)) * (GDN_DK ** -0.5)
    k = l2norm(qkv_c[..., GDN_QK:2 * GDN_QK].reshape(B, L, GDN_HEADS, GDN_DK))
    v = qkv_c[..., 2 * GDN_QK:].reshape(B, L, GDN_HEADS, GDN_DV).astype(F32)
    g = -jnp.exp(A_log.astype(F32)) * jax.nn.softplus(a + dt_bias.astype(F32))
    beta = jax.nn.sigmoid(b)
    S = S0.astype(F32)
    outs = []
    start = 0
    for seg in segments:
        sl = slice(start, start + seg)
        o_seg, S = gdn_segment(q[:, sl], k[:, sl], v[:, sl], g[:, sl], beta[:, sl], S)
        outs.append(o_seg)
        start += seg
    o = jnp.concatenate(outs, axis=1) if len(outs) > 1 else outs[0]
    o = o * lax.rsqrt(jnp.mean(o * o, axis=-1, keepdims=True) + RMS_EPS) * norm_w.astype(F32)
    o = o * jax.nn.silu(z.reshape(B, L, GDN_HEADS, GDN_DV).astype(F32))
    y = jnp.einsum('ble,ed->bld', o.reshape(B, L, GDN_V).astype(h.dtype), w_out)
    return y, new_buf.astype(conv_buf.dtype), S.astype(S0.dtype)


def rwkv_step(S, xs):
    r_t, w_t, k_t, v_t, kk_t, a_t = xs
    sa = jnp.einsum('bhvk,bhk->bhv', S, kk_t)
    S = (S * w_t[:, :, None, :] - sa[..., None] * (kk_t * a_t)[:, :, None, :]
         + v_t[..., None] * k_t[:, :, None, :])
    y = jnp.einsum('bhvk,bhk->bhv', S, r_t)
    return S, y


def rwkv_mixer(h, shift, S0, v_first, vres, mix, wr, wk, wv, wo, w0, w1, w2, a0, a1, a2,
               g1, g2, k_k, k_a, r_k, ln_w, ln_b):
    B, L, D = h.shape
    prev = jnp.concatenate([shift[:, None].astype(h.dtype), h[:, :-1]], axis=1)
    xx = prev - h
    xr = h + xx * mix[0]
    xw = h + xx * mix[1]
    xk = h + xx * mix[2]
    xv = h + xx * mix[3]
    xa = h + xx * mix[4]
    xg = h + xx * mix[5]
    r = xr @ wr
    k = xk @ wk
    v = xv @ wv
    w_log = -jax.nn.softplus(-(w0 + jnp.tanh(xw @ w1) @ w2).astype(F32)) - 0.5
    decay = jnp.exp(-jnp.exp(w_log))
    if vres is None:
        v_first = v
    else:
        v0, v1, v2 = vres
        v = v + (v_first - v) * jax.nn.sigmoid(v0 + (xv @ v1) @ v2)
    a = jax.nn.sigmoid((a0 + (xa @ a1) @ a2).astype(F32))
    gate = jax.nn.sigmoid(xg @ g1) @ g2

    def heads(t):
        return t.reshape(B, L, RWKV_HEADS, RWKV_N).astype(F32)

    kk = l2norm(heads(k * k_k))
    kf = k.astype(F32) * (1.0 + (a - 1.0) * k_a.astype(F32))
    rh, kh, vh, ah, dh = heads(r), heads(kf), heads(v), heads(a), heads(decay)
    tm = lambda t: jnp.moveaxis(t, 1, 0)
    S, y = lax.scan(rwkv_step, S0.astype(F32), (tm(rh), tm(dh), tm(kh), tm(vh), tm(kk), tm(ah)))
    y = jnp.moveaxis(y, 0, 1)
    mu = jnp.mean(y, axis=-1, keepdims=True)
    var = jnp.mean(jnp.square(y - mu), axis=-1, keepdims=True)
    y = ((y - mu) * lax.rsqrt(var + GN_EPS)).reshape(B, L, D) * ln_w.astype(F32) + ln_b.astype(F32)
    bonus = jnp.sum(rh * kh * r_k.astype(F32), axis=-1, keepdims=True) * vh
    y = y + bonus.reshape(B, L, D)
    out = (y * gate.astype(F32)).astype(h.dtype) @ wo
    return out, h[:, -1].astype(shift.dtype), S.astype(S0.dtype), v_first


def conv_ffn(h, buf, w_up, conv_w, w_down):
    up2 = h @ w_up
    gt, up = up2[..., :D_FF], up2[..., D_FF:]
    gc, new_buf = causal_dwconv(buf, gt, conv_w)
    y = (jax.nn.silu(gc) * up) @ w_down
    return y, new_buf.astype(buf.dtype)


def trunk(x, gdn_S, gdn_conv, rwkv_S, rwkv_shift, ffn_conv, segments, p):
    gS, gC, rS, rSh, fC = [], [], [], [], []
    v_first = None
    for i in range(DEPTH):
        hn = rmsnorm(x, p['norm_mix'][i])
        j = i // N_MIXERS
        if i % N_MIXERS == 0:
            y, cb, S = gdn_mixer(hn, gdn_conv[j], gdn_S[j], p['gdn_w_in'][j], p['gdn_conv_w'][j],
                                 p['gdn_A_log'][j], p['gdn_dt_bias'][j], p['gdn_norm_w'][j],
                                 p['gdn_w_out'][j], segments)
            gS.append(S)
            gC.append(cb)
        else:
            vres = None if j == 0 else (p['rwkv_v0'][j - 1], p['rwkv_v1'][j - 1], p['rwkv_v2'][j - 1])
            y, sh, S, v_first = rwkv_mixer(
                hn, rwkv_shift[j], rwkv_S[j], v_first, vres, p['rwkv_mix'][j], p['rwkv_wr'][j],
                p['rwkv_wk'][j], p['rwkv_wv'][j], p['rwkv_wo'][j], p['rwkv_w0'][j], p['rwkv_w1'][j],
                p['rwkv_w2'][j], p['rwkv_a0'][j], p['rwkv_a1'][j], p['rwkv_a2'][j], p['rwkv_g1'][j],
                p['rwkv_g2'][j], p['rwkv_k_k'][j], p['rwkv_k_a'][j], p['rwkv_r_k'][j],
                p['rwkv_ln_w'][j], p['rwkv_ln_b'][j])
            rS.append(S)
            rSh.append(sh)
        x = x + y
        hn = rmsnorm(x, p['norm_ffn'][i])
        y, fb = conv_ffn(hn, ffn_conv[i], p['ffn_w_up'][i], p['ffn_conv_w'][i], p['ffn_w_down'][i])
        fC.append(fb)
        x = x + y
    x = rmsnorm(x, p['norm_final'])
    return x, jnp.stack(gS), jnp.stack(gC), jnp.stack(rS), jnp.stack(rSh), jnp.stack(fC)


def setup_inputs(seed: int = 0) -> dict:
    key = jax.random.key(seed)
    ks = iter(jax.random.split(key, 64))
    D = D_MODEL

    def nrm(shape, scale):
        return jax.random.normal(next(ks), shape, F32) * scale

    def gain(shape):
        return 1.0 + nrm(shape, 0.02)

    def unif(shape, lo, hi):
        return jax.random.uniform(next(ks), shape, F32, lo, hi)

    dt = jnp.exp(unif((N_GDN, GDN_HEADS), math.log(1e-3), math.log(1e-1)))
    gdn_dt_bias = dt + jnp.log(-jnp.expm1(-dt))
    return {
        'x_prompt': nrm((BATCH, SEQ, D), 1.0),
        'x_sample': nrm((DEC_BATCH, DEC_SEQ, D), 1.0),
        'state_gdn_S': nrm((N_GDN, DEC_BATCH, GDN_HEADS, GDN_DK, GDN_DV), 0.1),
        'state_gdn_conv': nrm((N_GDN, DEC_BATCH, GDN_CONV - 1, GDN_CONV_DIM), 1.0),
        'state_rwkv_S': nrm((N_RWKV, DEC_BATCH, RWKV_HEADS, RWKV_N, RWKV_N), 0.3),
        'state_rwkv_shift': nrm((N_RWKV, DEC_BATCH, D), 1.0),
        'state_ffn_conv': nrm((DEPTH, DEC_BATCH, FFN_CONV - 1, D_FF), 1.0),
        'meta_tokens': nrm((N_META, D), 1.0),
        'norm_mix': gain((DEPTH, D)),
        'norm_ffn': gain((DEPTH, D)),
        'norm_final': gain((D,)),
        'gdn_w_in': nrm((N_GDN, D, GDN_IN), D ** -0.5),
        'gdn_conv_w': nrm((N_GDN, GDN_CONV, GDN_CONV_DIM), GDN_CONV ** -0.5),
        'gdn_A_log': jnp.log(unif((N_GDN, GDN_HEADS), 1.0, 16.0)),
        'gdn_dt_bias': gdn_dt_bias,
        'gdn_norm_w': gain((N_GDN, GDN_DV)),
        'gdn_w_out': nrm((N_GDN, GDN_V, D), GDN_V ** -0.5),
        'rwkv_mix': unif((N_RWKV, 6, D), 0.0, 1.0),
        'rwkv_wr': nrm((N_RWKV, D, D), D ** -0.5),
        'rwkv_wk': nrm((N_RWKV, D, D), D ** -0.5),
        'rwkv_wv': nrm((N_RWKV, D, D), D ** -0.5),
        'rwkv_wo': nrm((N_RWKV, D, D), D ** -0.5),
        'rwkv_w0': unif((N_RWKV, D), -4.0, 0.0),
        'rwkv_w1': nrm((N_RWKV, D, DECAY_LORA), D ** -0.5),
        'rwkv_w2': nrm((N_RWKV, DECAY_LORA, D), 0.1 * DECAY_LORA ** -0.5),
        'rwkv_a0': nrm((N_RWKV, D), 0.1),
        'rwkv_a1': nrm((N_RWKV, D, AAA_LORA), D ** -0.5),
        'rwkv_a2': nrm((N_RWKV, AAA_LORA, D), 0.1 * AAA_LORA ** -0.5),
        'rwkv_g1': nrm((N_RWKV, D, GATE_LORA), D ** -0.5),
        'rwkv_g2': nrm((N_RWKV, GATE_LORA, D), GATE_LORA ** -0.5),
        'rwkv_k_k': 0.85 + nrm((N_RWKV, D), 0.02),
        'rwkv_k_a': gain((N_RWKV, D)),
        'rwkv_r_k': nrm((N_RWKV, RWKV_HEADS, RWKV_N), 0.1),
        'rwkv_ln_w': gain((N_RWKV, D)),
        'rwkv_ln_b': nrm((N_RWKV, D), 0.01),
        'rwkv_v0': nrm((N_RWKV - 1, D), 0.1),
        'rwkv_v1': nrm((N_RWKV - 1, D, MV_LORA), D ** -0.5),
        'rwkv_v2': nrm((N_RWKV - 1, MV_LORA, D), 0.1 * MV_LORA ** -0.5),
        'ffn_w_up': nrm((DEPTH, D, 2 * D_FF), D ** -0.5),
        'ffn_conv_w': nrm((DEPTH, FFN_CONV, D_FF), FFN_CONV ** -0.5),
        'ffn_w_down': nrm((DEPTH, D_FF, D), D_FF ** -0.5),
    }


def reference(x_prompt, x_sample, state_gdn_S, state_gdn_conv, state_rwkv_S, state_rwkv_shift,
              state_ffn_conv, meta_tokens, norm_mix, norm_ffn, norm_final, gdn_w_in, gdn_conv_w,
              gdn_A_log, gdn_dt_bias, gdn_norm_w, gdn_w_out, rwkv_mix, rwkv_wr, rwkv_wk, rwkv_wv,
              rwkv_wo, rwkv_w0, rwkv_w1, rwkv_w2, rwkv_a0, rwkv_a1, rwkv_a2, rwkv_g1, rwkv_g2,
              rwkv_k_k, rwkv_k_a, rwkv_r_k, rwkv_ln_w, rwkv_ln_b, rwkv_v0, rwkv_v1, rwkv_v2,
              ffn_w_up, ffn_conv_w, ffn_w_down):
    p = dict(norm_mix=norm_mix, norm_ffn=norm_ffn, norm_final=norm_final, gdn_w_in=gdn_w_in,
             gdn_conv_w=gdn_conv_w, gdn_A_log=gdn_A_log, gdn_dt_bias=gdn_dt_bias,
             gdn_norm_w=gdn_norm_w, gdn_w_out=gdn_w_out, rwkv_mix=rwkv_mix, rwkv_wr=rwkv_wr,
             rwkv_wk=rwkv_wk, rwkv_wv=rwkv_wv, rwkv_wo=rwkv_wo, rwkv_w0=rwkv_w0, rwkv_w1=rwkv_w1,
             rwkv_w2=rwkv_w2, rwkv_a0=rwkv_a0, rwkv_a1=rwkv_a1, rwkv_a2=rwkv_a2, rwkv_g1=rwkv_g1,
             rwkv_g2=rwkv_g2, rwkv_k_k=rwkv_k_k, rwkv_k_a=rwkv_k_a, rwkv_r_k=rwkv_r_k,
             rwkv_ln_w=rwkv_ln_w, rwkv_ln_b=rwkv_ln_b, rwkv_v0=rwkv_v0, rwkv_v1=rwkv_v1,
             rwkv_v2=rwkv_v2, ffn_w_up=ffn_w_up, ffn_conv_w=ffn_conv_w, ffn_w_down=ffn_w_down)
    B = x_prompt.shape[0]
    dt = x_prompt.dtype
    meta = jnp.broadcast_to(meta_tokens.astype(dt)[None], (B, N_META, D_MODEL))
    xp = jnp.concatenate([meta, x_prompt], axis=1)
    Lp = xp.shape[1]
    z_gS = jnp.zeros((N_GDN, B, GDN_HEADS, GDN_DK, GDN_DV), dt)
    z_gC = jnp.zeros((N_GDN, B, GDN_CONV - 1, GDN_CONV_DIM), dt)
    z_rS = jnp.zeros((N_RWKV, B, RWKV_HEADS, RWKV_N, RWKV_N), dt)
    z_rSh = jnp.zeros((N_RWKV, B, D_MODEL), dt)
    z_fC = jnp.zeros((DEPTH, B, FFN_CONV - 1, D_FF), dt)
    yp, gS_p, gC_p, rS_p, rSh_p, fC_p = trunk(xp, z_gS, z_gC, z_rS, z_rSh, z_fC,
                                               [N_META, Lp - N_META], p)
    y_prompt = yp[:, N_META:]
    y_sample, gS_s, gC_s, rS_s, rSh_s, fC_s = trunk(x_sample, state_gdn_S, state_gdn_conv,
                                                     state_rwkv_S, state_rwkv_shift,
                                                     state_ffn_conv, [x_sample.shape[1]], p)
    return (y_prompt, y_sample, gS_p, gC_p, rS_p, rSh_p, fC_p, gS_s, gC_s, rS_s, rSh_s, fC_s)
```

```python
import functools

import jax
import jax.numpy as jnp
from jax import lax
from jax.experimental import pallas as pl
from jax.experimental.pallas import tpu as pltpu

F32 = jnp.float32
BF16 = jnp.bfloat16
HI = lax.Precision.HIGHEST

N_META = 16
GDN_HEADS = 8
GDN_DK = 128
GDN_DV = 128
RWKV_N = 64
RMS_EPS = 1e-6
L2_EPS = 1e-6
GN_EPS = 64e-5

PROMPT_CHUNK = 64
SAMPLE_CHUNK = 8
VMEM_LIMIT_BYTES = 48 * 1024 * 1024


def _mm_kernel(x_ref, w_ref, o_ref):
    o_ref[...] = jnp.dot(x_ref[...].astype(BF16), w_ref[...], preferred_element_type=F32)


def _pick_tile(n, cands):
    for c in cands:
        if n % c == 0:
            return c
    return n


def matmul(x, w):
    M, K = x.shape
    N = w.shape[1]
    tm = _pick_tile(M, (512, 256, 128, 64, 32, 16, 8))
    tn = _pick_tile(N, (512, 256, 128))
    return pl.pallas_call(
        _mm_kernel,
        grid=(N // tn, M // tm),
        in_specs=[pl.BlockSpec((tm, K), lambda j, i: (i, 0)),
                  pl.BlockSpec((K, tn), lambda j, i: (0, j))],
        out_specs=pl.BlockSpec((tm, tn), lambda j, i: (i, j)),
        out_shape=jax.ShapeDtypeStruct((M, N), F32),
        compiler_params=pltpu.CompilerParams(
            dimension_semantics=("parallel", "parallel"),
            vmem_limit_bytes=VMEM_LIMIT_BYTES),
        name="matmul",
    )(x, w)


def _dot(a, b):
    return jnp.dot(a, b, preferred_element_type=F32, precision=HI)


def _dot_nt(a, b):
    return lax.dot_general(a, b, (((1,), (1,)), ((), ())), precision=HI,
                           preferred_element_type=F32)


def _dot_tn(a, b):
    return lax.dot_general(a, b, (((0,), (0,)), ((), ())), precision=HI,
                           preferred_element_type=F32)


def _unit_lower_solve(n_strict, rhs, size):
    x = rhs - _dot(n_strict, rhs)
    p = n_strict
    span = 2
    while span < size:
        p = _dot(p, p)
        x = x + _dot(p, x)
        span *= 2
    return x


def _gdn_kernel(q_ref, k_ref, v_ref, gcol_ref, bcol_ref, grow_ref, s0_ref,
                o_ref, sout_ref, s_scr, *, chunk, heads):
    h = pl.program_id(1)
    c = pl.program_id(2)

    @pl.when(c == 0)
    def _():
        s_scr[...] = s0_ref[0, 0]

    q = q_ref[0]
    k = k_ref[0]
    v = v_ref[0]
    onehot = (lax.broadcasted_iota(jnp.int32, (1, heads), 1) == h).astype(F32)
    g_col = jnp.sum(gcol_ref[0] * onehot, axis=1, keepdims=True)
    beta = jnp.sum(bcol_ref[0] * onehot, axis=1, keepdims=True)
    g_row = grow_ref[0, 0, pl.ds(h, 1), :]

    ri = lax.broadcasted_iota(jnp.int32, (chunk, chunk), 0)
    ci = lax.broadcasted_iota(jnp.int32, (chunk, chunk), 1)
    causal = ri >= ci
    strict = ri > ci
    G_col = jnp.sum(jnp.where(causal, g_row, 0.0), axis=1, keepdims=True)
    G_row = jnp.sum(jnp.where(ri <= ci, g_col, 0.0), axis=0, keepdims=True)
    g_last = jnp.sum(g_row, axis=1, keepdims=True)
    decay = jnp.where(causal, jnp.exp(jnp.minimum(G_col - G_row, 0.0)), 0.0)

    kb = k * beta
    kk = _dot_nt(kb, k) * decay
    n_strict = jnp.where(strict, kk, 0.0)
    eG = jnp.exp(G_col)
    rhs = jnp.concatenate([v * beta, kb * eG], axis=1)
    sol = _unit_lower_solve(n_strict, rhs, chunk)
    u = sol[:, :GDN_DV]
    w = sol[:, GDN_DV:]
    qk = _dot_nt(q, k) * decay

    S = s_scr[...]
    v_new = u - _dot(w, S)
    o = _dot(q * eG, S) + _dot(qk, v_new)
    k_dec = k * jnp.exp(g_last - G_col)
    S_new = S * jnp.exp(g_last) + _dot_tn(k_dec, v_new)
    s_scr[...] = S_new
    o_ref[0] = o

    @pl.when(c == pl.num_programs(2) - 1)
    def _():
        sout_ref[0, 0] = S_new


def gdn_recurrence(qkv, g, beta, S0, chunk):
    B, L, _ = qkv.shape
    H = GDN_HEADS
    n = L // chunk
    g_row = jnp.transpose(g, (0, 2, 1)).reshape(B, H, n, chunk).transpose(0, 2, 1, 3)
    blk = lambda off: pl.BlockSpec((1, chunk, GDN_DK), lambda b, h, c, off=off: (b, c, off + h))
    col = pl.BlockSpec((1, chunk, H), lambda b, h, c: (b, c, 0))
    o, S = pl.pallas_call(
        functools.partial(_gdn_kernel, chunk=chunk, heads=H),
        grid=(B, H, n),
        in_specs=[blk(0), blk(H), blk(2 * H), col, col,
                  pl.BlockSpec((1, 1, H, chunk), lambda b, h, c: (b, c, 0, 0)),
                  pl.BlockSpec((1, 1, GDN_DK, GDN_DV), lambda b, h, c: (b, h, 0, 0))],
        out_specs=[pl.BlockSpec((1, chunk, GDN_DV), lambda b, h, c: (b, c, h)),
                   pl.BlockSpec((1, 1, GDN_DK, GDN_DV), lambda b, h, c: (b, h, 0, 0))],
        out_shape=[jax.ShapeDtypeStruct((B, L, H * GDN_DV), F32),
                   jax.ShapeDtypeStruct(S0.shape, F32)],
        scratch_shapes=[pltpu.VMEM((GDN_DK, GDN_DV), F32)],
        compiler_params=pltpu.CompilerParams(
            dimension_semantics=("parallel", "parallel", "arbitrary"),
            vmem_limit_bytes=VMEM_LIMIT_BYTES),
        name="gdn_recurrence",
    )(qkv, qkv, qkv, g, beta, g_row, S0)
    return o, S


def _rwkv_kernel(r_ref, lw_ref, k_ref, v_ref, kk_ref, b_ref, s0_ref,
                 o_ref, sout_ref, s_scr, *, chunk):
    c = pl.program_id(2)

    @pl.when(c == 0)
    def _():
        s_scr[...] = s0_ref[0]

    ri = lax.broadcasted_iota(jnp.int32, (chunk, chunk), 0)
    ci = lax.broadcasted_iota(jnp.int32, (chunk, chunk), 1)
    incl = (ri >= ci).astype(F32)
    ri2 = lax.broadcasted_iota(jnp.int32, (2 * chunk, 2 * chunk), 0)
    ci2 = lax.broadcasted_iota(jnp.int32, (2 * chunk, 2 * chunk), 1)
    cj = jnp.where(ci2 >= chunk, ci2 - chunk, ci2)
    keep = ((ri2 < chunk) & (ri2 > cj)) | ((ri2 >= chunk) & ((ri2 - chunk) >= cj))

    outs = []
    for j in range(2):
        sl = slice(j * RWKV_N, (j + 1) * RWKV_N)
        r = r_ref[0][:, sl]
        lw = lw_ref[0][:, sl]
        k = k_ref[0][:, sl]
        v = v_ref[0][:, sl]
        kk = kk_ref[0][:, sl]
        b = b_ref[0][:, sl]
        S = s_scr[j]

        LP = _dot(incl, lw)
        LP_last = LP[chunk - 1:chunk, :]
        P = jnp.exp(LP)
        P_inv = jnp.exp(-LP)
        rows = jnp.concatenate([kk * jnp.exp(LP - lw), r * P], axis=0)
        cols = jnp.concatenate([k * P_inv, b * P_inv], axis=0)
        A = jnp.where(keep, _dot_nt(rows, cols), 0.0)
        A_kk = A[:chunk, :chunk]
        A_kb = A[:chunk, chunk:]
        A_rk = A[chunk:, :chunk]
        A_rb = A[chunk:, chunk:]
        base = _dot_nt(rows, S)
        u = _unit_lower_solve(A_kb, base[:chunk] + _dot(A_kk, v), chunk)
        o = base[chunk:] + _dot(A_rk, v) - _dot(A_rb, u)
        dec = jnp.exp(LP_last - LP)
        S_new = S * jnp.exp(LP_last) + _dot_tn(v, k * dec) - _dot_tn(u, b * dec)
        s_scr[j] = S_new
        outs.append(o)
    o_ref[0] = jnp.concatenate(outs, axis=1)

    @pl.when(c == pl.num_programs(2) - 1)
    def _():
        sout_ref[0] = s_scr[...]


def rwkv_recurrence(r, lw, k, v, kk, b, S0, chunk):
    B, L, D = r.shape
    n = L // chunk
    pairs = D // (2 * RWKV_N)
    blk = pl.BlockSpec((1, chunk, 2 * RWKV_N), lambda bb, p, c: (bb, c, p))
    st = pl.BlockSpec((1, 2, RWKV_N, RWKV_N), lambda bb, p, c: (bb, p, 0, 0))
    o, S = pl.pallas_call(
        functools.partial(_rwkv_kernel, chunk=chunk),
        grid=(B, pairs, n),
        in_specs=[blk] * 6 + [st],
        out_specs=[blk, st],
        out_shape=[jax.ShapeDtypeStruct((B, L, D), F32),
                   jax.ShapeDtypeStruct(S0.shape, F32)],
        scratch_shapes=[pltpu.VMEM((2, RWKV_N, RWKV_N), F32)],
        compiler_params=pltpu.CompilerParams(
            dimension_semantics=("parallel", "parallel", "arbitrary"),
            vmem_limit_bytes=VMEM_LIMIT_BYTES),
        name="rwkv_recurrence",
    )(r, lw, k, v, kk, b, S0)
    return o, S


def _rmsnorm(x, g):
    return x * lax.rsqrt(jnp.mean(x * x, axis=-1, keepdims=True) + RMS_EPS) * g


def _l2norm(x):
    return x * lax.rsqrt(jnp.sum(x * x, axis=-1, keepdims=True) + L2_EPS)


def _causal_dwconv(buf, x, w, valid):
    width = w.shape[0]
    L = x.shape[1]
    xc = jnp.concatenate([buf, x], axis=1)
    out = xc[:, 0:L] * w[0]
    for j in range(1, width):
        out = out + xc[:, j:j + L] * w[j]
    return out, xc[:, valid:valid + width - 1]


def _proj(x, w):
    B, L, K = x.shape
    return matmul(x.reshape(B * L, K), w).reshape(B, L, w.shape[1])


def _gdn_mixer(hn, conv_buf, S0, p, j, mask, valid, chunk):
    B, L, _ = hn.shape
    H = GDN_HEADS
    qk_dim = H * GDN_DK
    v_dim = H * GDN_DV
    conv_dim = 2 * qk_dim + v_dim
    proj = _proj(hn, p['gdn_w_main'][j])
    ab = _proj(hn, p['gdn_w_ab'][j])
    qkv = proj[..., :conv_dim]
    z = proj[..., conv_dim:]
    a = ab[..., :H]
    b = ab[..., H:2 * H]
    qkv_c, new_buf = _causal_dwconv(conv_buf, qkv, p['gdn_conv_w'][j], valid)
    qkv_c = jax.nn.silu(qkv_c)
    q = _l2norm(qkv_c[..., :qk_dim].reshape(B, L, H, GDN_DK)) * (GDN_DK ** -0.5)
    k = _l2norm(qkv_c[..., qk_dim:2 * qk_dim].reshape(B, L, H, GDN_DK))
    v = qkv_c[..., 2 * qk_dim:]
    qkv_n = jnp.concatenate([q.reshape(B, L, qk_dim), k.reshape(B, L, qk_dim), v], axis=-1)
    g = -jnp.exp(p['gdn_A_log'][j]) * jax.nn.softplus(a + p['gdn_dt_bias'][j])
    beta = jax.nn.sigmoid(b)
    m = mask[None, :, None]
    o, S = gdn_recurrence(qkv_n, g * m, beta * m, S0, chunk)
    o = o.reshape(B, L, H, GDN_DV)
    o = o * lax.rsqrt(jnp.mean(o * o, axis=-1, keepdims=True) + RMS_EPS) * p['gdn_norm_w'][j]
    o = o * jax.nn.silu(z.reshape(B, L, H, GDN_DV))
    y = _proj(o.reshape(B, L, v_dim), p['gdn_w_out'][j])
    return y, new_buf, S


def _rwkv_mixer(hn, shift, S0, v_first, p, j, mask, valid, chunk):
    B, L, D = hn.shape
    Hh = D // RWKV_N
    prev = jnp.concatenate([shift[:, None], hn[:, :-1]], axis=1)
    xx = prev - hn
    mix = p['rwkv_mix'][j]
    xr = hn + xx * mix[0]
    xw = hn + xx * mix[1]
    xk = hn + xx * mix[2]
    xv = hn + xx * mix[3]
    xa = hn + xx * mix[4]
    xg = hn + xx * mix[5]
    r = _proj(xr, p['rwkv_wr'][j])
    k = _proj(xk, p['rwkv_wk'][j])
    v = _proj(xv, p['rwkv_wv'][j])
    w_log = -jax.nn.softplus(-(p['rwkv_w0'][j] + _proj(jnp.tanh(_proj(xw, p['rwkv_w1'][j])), p['rwkv_w2'][j]))) - 0.5
    log_decay = -jnp.exp(w_log)
    if j == 0:
        v_first = v
    else:
        v = v + (v_first - v) * jax.nn.sigmoid(
            p['rwkv_v0'][j - 1] + _proj(_proj(xv, p['rwkv_v1'][j - 1]), p['rwkv_v2'][j - 1]))
    a = jax.nn.sigmoid(p['rwkv_a0'][j] + _proj(_proj(xa, p['rwkv_a1'][j]), p['rwkv_a2'][j]))
    gate = _proj(jax.nn.sigmoid(_proj(xg, p['rwkv_g1'][j])), p['rwkv_g2'][j])

    kk = _l2norm((k * p['rwkv_k_k'][j]).reshape(B, L, Hh, RWKV_N)).reshape(B, L, D)
    kf = k * (1.0 + (a - 1.0) * p['rwkv_k_a'][j])
    m = mask[None, :, None]
    kk_m = kk * m
    y, S = rwkv_recurrence(r, log_decay * m, kf * m, v * m, kk_m, kk_m * a, S0, chunk)
    y = y.reshape(B, L, Hh, RWKV_N)
    mu = jnp.mean(y, axis=-1, keepdims=True)
    var = jnp.mean(jnp.square(y - mu), axis=-1, keepdims=True)
    y = ((y - mu) * lax.rsqrt(var + GN_EPS)).reshape(B, L, D) * p['rwkv_ln_w'][j] + p['rwkv_ln_b'][j]
    rh = r.reshape(B, L, Hh, RWKV_N)
    kh = kf.reshape(B, L, Hh, RWKV_N)
    vh = v.reshape(B, L, Hh, RWKV_N)
    bonus = jnp.sum(rh * kh * p['rwkv_r_k'][j], axis=-1, keepdims=True) * vh
    y = y + bonus.reshape(B, L, D)
    out = _proj(y * gate, p['rwkv_wo'][j])
    return out, hn[:, valid - 1], S, v_first


def _conv_ffn(hn, buf, p, i, valid):
    d_ff = p['ffn_conv_w'].shape[-1]
    up2 = _proj(hn, p['ffn_w_up'][i])
    gt, up = up2[..., :d_ff], up2[..., d_ff:]
    gc, new_buf = _causal_dwconv(buf, gt, p['ffn_conv_w'][i], valid)
    y = _proj(jax.nn.silu(gc) * up, p['ffn_w_down'][i])
    return y, new_buf


def _trunk(x, gdn_S, gdn_conv, rwkv_S, rwkv_shift, ffn_conv, p, valid, chunk):
    L = x.shape[1]
    depth = p['norm_mix'].shape[0]
    mask = (jnp.arange(L) < valid).astype(F32)
    gS, gC, rS, rSh, fC = [], [], [], [], []
    v_first = None
    for i in range(depth):
        hn = _rmsnorm(x, p['norm_mix'][i])
        j = i // 2
        if i % 2 == 0:
            y, cb, S = _gdn_mixer(hn, gdn_conv[j], gdn_S[j], p, j, mask, valid, chunk)
            gS.append(S)
            gC.append(cb)
        else:
            y, sh, S, v_first = _rwkv_mixer(hn, rwkv_shift[j], rwkv_S[j], v_first, p, j, mask, valid, chunk)
            rS.append(S)
            rSh.append(sh)
        x = x + y
        hn = _rmsnorm(x, p['norm_ffn'][i])
        y, fb = _conv_ffn(hn, ffn_conv[i], p, i, valid)
        fC.append(fb)
        x = x + y
    x = _rmsnorm(x, p['norm_final'])
    return x, jnp.stack(gS), jnp.stack(gC), jnp.stack(rS), jnp.stack(rSh), jnp.stack(fC)


def _pad_tokens(x, chunk):
    L = x.shape[1]
    Lp = -(-L // chunk) * chunk
    return jnp.pad(x, ((0, 0), (0, Lp - L), (0, 0)))


def kernel(x_prompt, x_sample, state_gdn_S, state_gdn_conv, state_rwkv_S, state_rwkv_shift, state_ffn_conv, meta_tokens, norm_mix, norm_ffn, norm_final, gdn_w_in, gdn_conv_w, gdn_A_log, gdn_dt_bias, gdn_norm_w, gdn_w_out, rwkv_mix, rwkv_wr, rwkv_wk, rwkv_wv, rwkv_wo, rwkv_w0, rwkv_w1, rwkv_w2, rwkv_a0, rwkv_a1, rwkv_a2, rwkv_g1, rwkv_g2, rwkv_k_k, rwkv_k_a, rwkv_r_k, rwkv_ln_w, rwkv_ln_b, rwkv_v0, rwkv_v1, rwkv_v2, ffn_w_up, ffn_conv_w, ffn_w_down):
    n_main = gdn_w_in.shape[-1] - 2 * GDN_HEADS
    w_ab = jnp.pad(gdn_w_in[..., n_main:], ((0, 0), (0, 0), (0, 128 - 2 * GDN_HEADS)))
    bf = lambda w: w.astype(BF16)
    p = dict(norm_mix=norm_mix, norm_ffn=norm_ffn, norm_final=norm_final,
             gdn_w_main=bf(gdn_w_in[..., :n_main]), gdn_w_ab=bf(w_ab),
             gdn_conv_w=gdn_conv_w, gdn_A_log=gdn_A_log, gdn_dt_bias=gdn_dt_bias,
             gdn_norm_w=gdn_norm_w, gdn_w_out=bf(gdn_w_out), rwkv_mix=rwkv_mix, rwkv_wr=bf(rwkv_wr),
             rwkv_wk=bf(rwkv_wk), rwkv_wv=bf(rwkv_wv), rwkv_wo=bf(rwkv_wo), rwkv_w0=rwkv_w0,
             rwkv_w1=bf(rwkv_w1), rwkv_w2=bf(rwkv_w2), rwkv_a0=rwkv_a0, rwkv_a1=bf(rwkv_a1),
             rwkv_a2=bf(rwkv_a2), rwkv_g1=bf(rwkv_g1), rwkv_g2=bf(rwkv_g2), rwkv_k_k=rwkv_k_k,
             rwkv_k_a=rwkv_k_a, rwkv_r_k=rwkv_r_k.reshape(rwkv_r_k.shape[0], 1, 1, -1, RWKV_N),
             rwkv_ln_w=rwkv_ln_w, rwkv_ln_b=rwkv_ln_b, rwkv_v0=rwkv_v0, rwkv_v1=bf(rwkv_v1),
             rwkv_v2=bf(rwkv_v2), ffn_w_up=bf(ffn_w_up), ffn_conv_w=ffn_conv_w, ffn_w_down=bf(ffn_w_down))

    B = x_prompt.shape[0]
    D = x_prompt.shape[-1]
    meta = jnp.broadcast_to(meta_tokens[None], (B, N_META, D))
    xp = jnp.concatenate([meta, x_prompt], axis=1)
    Lp = xp.shape[1]
    zeros_like_batch = lambda s: jnp.zeros((s.shape[0], B) + s.shape[2:], F32)
    yp, gS_p, gC_p, rS_p, rSh_p, fC_p = _trunk(
        _pad_tokens(xp, PROMPT_CHUNK), zeros_like_batch(state_gdn_S), zeros_like_batch(state_gdn_conv),
        zeros_like_batch(state_rwkv_S), zeros_like_batch(state_rwkv_shift), zeros_like_batch(state_ffn_conv),
        p, Lp, PROMPT_CHUNK)
    y_prompt = yp[:, N_META:Lp]

    Ls = x_sample.shape[1]
    ys, gS_s, gC_s, rS_s, rSh_s, fC_s = _trunk(
        _pad_tokens(x_sample, SAMPLE_CHUNK), state_gdn_S, state_gdn_conv, state_rwkv_S,
        state_rwkv_shift, state_ffn_conv, p, Ls, SAMPLE_CHUNK)
    y_sample = ys[:, :Ls]
    return (y_prompt, y_sample, gS_p, gC_p, rS_p, rSh_p, fC_p, gS_s, gC_s, rS_s, rSh_s, fC_s)
```

```python
import functools

import jax
import jax.numpy as jnp
from jax import lax
from jax.experimental import pallas as pl
from jax.experimental.pallas import tpu as pltpu

F32 = jnp.float32
BF16 = jnp.bfloat16

N_META = 16
GDN_HEADS = 8
GDN_DK = 128
GDN_DV = 128
RWKV_N = 64
RMS_EPS = 1e-6
L2_EPS = 1e-6
GN_EPS = 64e-5

PROMPT_CHUNK = 64
SAMPLE_CHUNK = 8
VMEM_LIMIT_BYTES = 48 * 1024 * 1024


def _mm_kernel(x_ref, w_ref, o_ref):
    o_ref[...] = jnp.dot(x_ref[...].astype(BF16), w_ref[...], preferred_element_type=F32)


def _pick_tile(n, cands):
    for c in cands:
        if n % c == 0:
            return c
    return n


def matmul(x, w):
    M, K = x.shape
    N = w.shape[1]
    tm = _pick_tile(M, (512, 256, 128, 64, 32, 16, 8))
    tn = _pick_tile(N, (512, 256, 128))
    return pl.pallas_call(
        _mm_kernel,
        grid=(N // tn, M // tm),
        in_specs=[pl.BlockSpec((tm, K), lambda j, i: (i, 0)),
                  pl.BlockSpec((K, tn), lambda j, i: (0, j))],
        out_specs=pl.BlockSpec((tm, tn), lambda j, i: (i, j)),
        out_shape=jax.ShapeDtypeStruct((M, N), F32),
        compiler_params=pltpu.CompilerParams(
            dimension_semantics=("parallel", "parallel"),
            vmem_limit_bytes=VMEM_LIMIT_BYTES),
        name="matmul",
    )(x, w)


def _dot(a, b):
    return jnp.dot(a.astype(BF16), b.astype(BF16), preferred_element_type=F32)


def _dot_nt(a, b):
    return lax.dot_general(a.astype(BF16), b.astype(BF16), (((1,), (1,)), ((), ())),
                           preferred_element_type=F32)


def _dot_tn(a, b):
    return lax.dot_general(a.astype(BF16), b.astype(BF16), (((0,), (0,)), ((), ())),
                           preferred_element_type=F32)


def _split2(x):
    hi = x.astype(BF16)
    return hi, (x - hi.astype(F32)).astype(BF16)


def _dot3(a, b):
    ah, al = _split2(a)
    bh, bl = _split2(b)
    d = lambda x, y: jnp.dot(x, y, preferred_element_type=F32)
    return d(ah, bh) + (d(ah, bl) + d(al, bh))


def _cumsum_rows(incl_bf16, x):
    hi = x.astype(BF16)
    r1 = x - hi.astype(F32)
    mid = r1.astype(BF16)
    lo = (r1 - mid.astype(F32)).astype(BF16)
    dot = lambda p: jnp.dot(incl_bf16, p, preferred_element_type=F32)
    return dot(hi) + dot(mid) + dot(lo)


def _gdn_kernel(q_ref, k_ref, v_ref, gcol_ref, bcol_ref, grow_ref, s0_ref,
                o_ref, sout_ref, s_scr, *, chunk, heads):
    c = pl.program_id(1)

    @pl.when(c == 0)
    def _():
        s_scr[...] = s0_ref[0]

    ri = lax.broadcasted_iota(jnp.int32, (chunk, chunk), 0)
    ci = lax.broadcasted_iota(jnp.int32, (chunk, chunk), 1)
    causal = ri >= ci
    strict = ri > ci
    hs = range(heads)
    sl = [slice(h * GDN_DK, (h + 1) * GDN_DK) for h in hs]
    q = [q_ref[0, :, sl[h]] for h in hs]
    k = [k_ref[0, :, sl[h]] for h in hs]
    v = [v_ref[0, :, sl[h]] for h in hs]
    g_col = [gcol_ref[0, :, h:h + 1] for h in hs]
    beta = [bcol_ref[0, :, h:h + 1] for h in hs]
    g_row = [grow_ref[0, 0, h:h + 1, :] for h in hs]
    G_col = [jnp.sum(jnp.where(causal, g_row[h], 0.0), axis=1, keepdims=True) for h in hs]
    G_row = [jnp.sum(jnp.where(ri <= ci, g_col[h], 0.0), axis=0, keepdims=True) for h in hs]
    g_last = [jnp.sum(g_row[h], axis=1, keepdims=True) for h in hs]
    decay = [jnp.where(causal, jnp.exp(jnp.minimum(G_col[h] - G_row[h], 0.0)), 0.0) for h in hs]
    eG = [jnp.exp(G_col[h]) for h in hs]
    k_bf = [k[h].astype(BF16) for h in hs]
    kb = [k[h] * beta[h] for h in hs]
    S = [s_scr[h] for h in hs]
    S_bf = [S[h].astype(BF16) for h in hs]
    n_strict = [jnp.where(strict, _dot_nt(kb[h], k_bf[h]) * decay[h], 0.0) for h in hs]
    qk = [(_dot_nt(q[h], k_bf[h]) * decay[h]).astype(BF16) for h in hs]
    qS = [_dot(q[h] * eG[h], S_bf[h]) for h in hs]
    x = [jnp.concatenate([v[h] * beta[h], kb[h] * eG[h]], axis=1) for h in hs]
    p = n_strict
    x = [x[h] - _dot3(p[h], x[h]) for h in hs]
    span = 2
    while span < chunk:
        p = [_dot3(p[h], p[h]) for h in hs]
        x = [x[h] + _dot3(p[h], x[h]) for h in hs]
        span *= 2
    v_new = [x[h][:, :GDN_DV] - _dot(x[h][:, GDN_DV:], S_bf[h]) for h in hs]
    v_new_bf = [v_new[h].astype(BF16) for h in hs]
    for h in hs:
        o_ref[0, :, sl[h]] = qS[h] + _dot(qk[h], v_new_bf[h])
    for h in hs:
        k_dec = k[h] * jnp.exp(g_last[h] - G_col[h])
        s_scr[h] = S[h] * jnp.exp(g_last[h]) + _dot_tn(k_dec, v_new_bf[h])

    @pl.when(c == pl.num_programs(1) - 1)
    def _():
        sout_ref[0] = s_scr[...]


def gdn_recurrence(qkv, g, beta, S0, chunk):
    B, L, _ = qkv.shape
    H = GDN_HEADS
    n = L // chunk
    g_row = jnp.transpose(g, (0, 2, 1)).reshape(B, H, n, chunk).transpose(0, 2, 1, 3)
    blk = lambda part: pl.BlockSpec((1, chunk, H * GDN_DK), lambda b, c, part=part: (b, c, part))
    col = pl.BlockSpec((1, chunk, H), lambda b, c: (b, c, 0))
    st = pl.BlockSpec((1, H, GDN_DK, GDN_DV), lambda b, c: (b, 0, 0, 0))
    o, S = pl.pallas_call(
        functools.partial(_gdn_kernel, chunk=chunk, heads=H),
        grid=(B, n),
        in_specs=[blk(0), blk(1), blk(2), col, col,
                  pl.BlockSpec((1, 1, H, chunk), lambda b, c: (b, c, 0, 0)), st],
        out_specs=[blk(0), st],
        out_shape=[jax.ShapeDtypeStruct((B, L, H * GDN_DV), F32),
                   jax.ShapeDtypeStruct(S0.shape, F32)],
        scratch_shapes=[pltpu.VMEM((H, GDN_DK, GDN_DV), F32)],
        compiler_params=pltpu.CompilerParams(
            dimension_semantics=("parallel", "arbitrary"),
            vmem_limit_bytes=VMEM_LIMIT_BYTES),
        name="gdn_recurrence",
    )(qkv, qkv, qkv, g, beta, g_row, S0)
    return o, S


def _rwkv_kernel(r_ref, lw_ref, k_ref, v_ref, kk_ref, b_ref, s0_ref,
                 o_ref, sout_ref, s_scr, *, chunk, heads):
    c = pl.program_id(1)

    @pl.when(c == 0)
    def _():
        s_scr[...] = s0_ref[0]

    ri = lax.broadcasted_iota(jnp.int32, (chunk, chunk), 0)
    ci = lax.broadcasted_iota(jnp.int32, (chunk, chunk), 1)
    incl = (ri >= ci).astype(BF16)
    ri2 = lax.broadcasted_iota(jnp.int32, (2 * chunk, 2 * chunk), 0)
    ci2 = lax.broadcasted_iota(jnp.int32, (2 * chunk, 2 * chunk), 1)
    cj = jnp.where(ci2 >= chunk, ci2 - chunk, ci2)
    keep = ((ri2 < chunk) & (ri2 > cj)) | ((ri2 >= chunk) & ((ri2 - chunk) >= cj))

    lw = lw_ref[0]
    LP = _cumsum_rows(incl, lw)
    LP_last = LP[chunk - 1:chunk, :]
    P = jnp.exp(LP)
    P_inv = jnp.exp(-LP)
    dec = jnp.exp(LP_last - LP)
    P_last = jnp.exp(LP_last)
    kk = kk_ref[0]
    b = b_ref[0]
    k = k_ref[0]
    rows_k = (kk * jnp.exp(LP - lw)).astype(BF16)
    rows_r = (r_ref[0] * P).astype(BF16)
    cols_k = (k * P_inv).astype(BF16)
    cols_b = (b * P_inv).astype(BF16)
    k_dec = (k * dec).astype(BF16)
    b_dec = (b * dec).astype(BF16)
    v_bf = v_ref[0].astype(BF16)

    hs = range(heads)
    sl = [slice(h * RWKV_N, (h + 1) * RWKV_N) for h in hs]
    rows = [jnp.concatenate([rows_k[:, sl[h]], rows_r[:, sl[h]]], axis=0) for h in hs]
    cols = [jnp.concatenate([cols_k[:, sl[h]], cols_b[:, sl[h]]], axis=0) for h in hs]
    S = [s_scr[h] for h in hs]
    A = [jnp.where(keep, _dot_nt(rows[h], cols[h]), 0.0).astype(BF16) for h in hs]
    base = [_dot_nt(rows[h], S[h]) for h in hs]
    vh = [v_bf[:, sl[h]] for h in hs]
    x = [base[h][:chunk] + _dot(A[h][:chunk, :chunk], vh[h]) for h in hs]
    p = [A[h][:chunk, chunk:] for h in hs]
    x = [x[h] - _dot(p[h], x[h]) for h in hs]
    span = 2
    while span < chunk:
        p = [_dot(p[h], p[h]).astype(BF16) for h in hs]
        x = [x[h] + _dot(p[h], x[h]) for h in hs]
        span *= 2
    u = [x[h].astype(BF16) for h in hs]
    for h in hs:
        o_ref[0, :, sl[h]] = (base[h][chunk:] + _dot(A[h][chunk:, :chunk], vh[h])
                              - _dot(A[h][chunk:, chunk:], u[h]))
    for h in hs:
        s_scr[h] = (S[h] * P_last[:, sl[h]] + _dot_tn(vh[h], k_dec[:, sl[h]])
                    - _dot_tn(u[h], b_dec[:, sl[h]]))

    @pl.when(c == pl.num_programs(1) - 1)
    def _():
        sout_ref[0] = s_scr[...]


def rwkv_recurrence(r, lw, k, v, kk, b, S0, chunk):
    B, L, D = r.shape
    n = L // chunk
    heads = D // RWKV_N
    blk = pl.BlockSpec((1, chunk, D), lambda bb, c: (bb, c, 0))
    st = pl.BlockSpec((1, heads, RWKV_N, RWKV_N), lambda bb, c: (bb, 0, 0, 0))
    o, S = pl.pallas_call(
        functools.partial(_rwkv_kernel, chunk=chunk, heads=heads),
        grid=(B, n),
        in_specs=[blk] * 6 + [st],
        out_specs=[blk, st],
        out_shape=[jax.ShapeDtypeStruct((B, L, D), F32),
                   jax.ShapeDtypeStruct(S0.shape, F32)],
        scratch_shapes=[pltpu.VMEM((heads, RWKV_N, RWKV_N), F32)],
        compiler_params=pltpu.CompilerParams(
            dimension_semantics=("parallel", "arbitrary"),
            vmem_limit_bytes=VMEM_LIMIT_BYTES),
        name="rwkv_recurrence",
    )(r, lw, k, v, kk, b, S0)
    return o, S


def _rmsnorm(x, g):
    return x * lax.rsqrt(jnp.mean(x * x, axis=-1, keepdims=True) + RMS_EPS) * g


def _l2norm(x):
    return x * lax.rsqrt(jnp.sum(x * x, axis=-1, keepdims=True) + L2_EPS)


def _causal_dwconv(buf, x, w, valid):
    width = w.shape[0]
    L = x.shape[1]
    xc = jnp.concatenate([buf, x], axis=1)
    out = xc[:, 0:L] * w[0]
    for j in range(1, width):
        out = out + xc[:, j:j + L] * w[j]
    return out, xc[:, valid:valid + width - 1]


def _proj(x, w):
    B, L, K = x.shape
    return matmul(x.reshape(B * L, K), w).reshape(B, L, w.shape[1])


def _gdn_mixer(hn, conv_buf, S0, p, j, mask, valid, chunk):
    B, L, _ = hn.shape
    H = GDN_HEADS
    qk_dim = H * GDN_DK
    v_dim = H * GDN_DV
    conv_dim = 2 * qk_dim + v_dim
    proj = _proj(hn, p['gdn_w_main'][j])
    ab = _proj(hn, p['gdn_w_ab'][j])
    qkv = proj[..., :conv_dim]
    z = proj[..., conv_dim:]
    a = ab[..., :H]
    b = ab[..., H:2 * H]
    qkv_c, new_buf = _causal_dwconv(conv_buf, qkv, p['gdn_conv_w'][j], valid)
    qkv_c = jax.nn.silu(qkv_c)
    q = _l2norm(qkv_c[..., :qk_dim].reshape(B, L, H, GDN_DK)) * (GDN_DK ** -0.5)
    k = _l2norm(qkv_c[..., qk_dim:2 * qk_dim].reshape(B, L, H, GDN_DK))
    v = qkv_c[..., 2 * qk_dim:]
    qkv_n = jnp.concatenate([q.reshape(B, L, qk_dim), k.reshape(B, L, qk_dim), v], axis=-1)
    g = -jnp.exp(p['gdn_A_log'][j]) * jax.nn.softplus(a + p['gdn_dt_bias'][j])
    beta = jax.nn.sigmoid(b)
    m = mask[None, :, None]
    o, S = gdn_recurrence(qkv_n, g * m, beta * m, S0, chunk)
    o = o.reshape(B, L, H, GDN_DV)
    o = o * lax.rsqrt(jnp.mean(o * o, axis=-1, keepdims=True) + RMS_EPS) * p['gdn_norm_w'][j]
    o = o * jax.nn.silu(z.reshape(B, L, H, GDN_DV))
    y = _proj(o.reshape(B, L, v_dim), p['gdn_w_out'][j])
    return y, new_buf, S


def _rwkv_mixer(hn, shift, S0, v_first, p, j, mask, valid, chunk):
    B, L, D = hn.shape
    Hh = D // RWKV_N
    prev = jnp.concatenate([shift[:, None], hn[:, :-1]], axis=1)
    xx = prev - hn
    mix = p['rwkv_mix'][j]
    xr = hn + xx * mix[0]
    xw = hn + xx * mix[1]
    xk = hn + xx * mix[2]
    xv = hn + xx * mix[3]
    xa = hn + xx * mix[4]
    xg = hn + xx * mix[5]
    r = _proj(xr, p['rwkv_wr'][j])
    k = _proj(xk, p['rwkv_wk'][j])
    v = _proj(xv, p['rwkv_wv'][j])
    w_log = -jax.nn.softplus(-(p['rwkv_w0'][j] + _proj(jnp.tanh(_proj(xw, p['rwkv_w1'][j])), p['rwkv_w2'][j]))) - 0.5
    log_decay = -jnp.exp(w_log)
    if j == 0:
        v_first = v
    else:
        v = v + (v_first - v) * jax.nn.sigmoid(
            p['rwkv_v0'][j - 1] + _proj(_proj(xv, p['rwkv_v1'][j - 1]), p['rwkv_v2'][j - 1]))
    a = jax.nn.sigmoid(p['rwkv_a0'][j] + _proj(_proj(xa, p['rwkv_a1'][j]), p['rwkv_a2'][j]))
    gate = _proj(jax.nn.sigmoid(_proj(xg, p['rwkv_g1'][j])), p['rwkv_g2'][j])

    kk = _l2norm((k * p['rwkv_k_k'][j]).reshape(B, L, Hh, RWKV_N)).reshape(B, L, D)
    kf = k * (1.0 + (a - 1.0) * p['rwkv_k_a'][j])
    m = mask[None, :, None]
    kk_m = kk * m
    y, S = rwkv_recurrence(r, log_decay * m, kf * m, v * m, kk_m, kk_m * a, S0, chunk)
    y = y.reshape(B, L, Hh, RWKV_N)
    mu = jnp.mean(y, axis=-1, keepdims=True)
    var = jnp.mean(jnp.square(y - mu), axis=-1, keepdims=True)
    y = ((y - mu) * lax.rsqrt(var + GN_EPS)).reshape(B, L, D) * p['rwkv_ln_w'][j] + p['rwkv_ln_b'][j]
    rh = r.reshape(B, L, Hh, RWKV_N)
    kh = kf.reshape(B, L, Hh, RWKV_N)
    vh = v.reshape(B, L, Hh, RWKV_N)
    bonus = jnp.sum(rh * kh * p['rwkv_r_k'][j], axis=-1, keepdims=True) * vh
    y = y + bonus.reshape(B, L, D)
    out = _proj(y * gate, p['rwkv_wo'][j])
    return out, hn[:, valid - 1], S, v_first


def _conv_ffn(hn, buf, p, i, valid):
    d_ff = p['ffn_conv_w'].shape[-1]
    up2 = _proj(hn, p['ffn_w_up'][i])
    gt, up = up2[..., :d_ff], up2[..., d_ff:]
    gc, new_buf = _causal_dwconv(buf, gt, p['ffn_conv_w'][i], valid)
    y = _proj(jax.nn.silu(gc) * up, p['ffn_w_down'][i])
    return y, new_buf


def _trunk(x, gdn_S, gdn_conv, rwkv_S, rwkv_shift, ffn_conv, p, valid, chunk):
    L = x.shape[1]
    depth = p['norm_mix'].shape[0]
    mask = (jnp.arange(L) < valid).astype(F32)
    gS, gC, rS, rSh, fC = [], [], [], [], []
    v_first = None
    for i in range(depth):
        hn = _rmsnorm(x, p['norm_mix'][i])
        j = i // 2
        if i % 2 == 0:
            y, cb, S = _gdn_mixer(hn, gdn_conv[j], gdn_S[j], p, j, mask, valid, chunk)
            gS.append(S)
            gC.append(cb)
        else:
            y, sh, S, v_first = _rwkv_mixer(hn, rwkv_shift[j], rwkv_S[j], v_first, p, j, mask, valid, chunk)
            rS.append(S)
            rSh.append(sh)
        x = x + y
        hn = _rmsnorm(x, p['norm_ffn'][i])
        y, fb = _conv_ffn(hn, ffn_conv[i], p, i, valid)
        fC.append(fb)
        x = x + y
    x = _rmsnorm(x, p['norm_final'])
    return x, jnp.stack(gS), jnp.stack(gC), jnp.stack(rS), jnp.stack(rSh), jnp.stack(fC)


def _pad_tokens(x, chunk):
    L = x.shape[1]
    Lp = -(-L // chunk) * chunk
    return jnp.pad(x, ((0, 0), (0, Lp - L), (0, 0)))


def kernel(x_prompt, x_sample, state_gdn_S, state_gdn_conv, state_rwkv_S, state_rwkv_shift, state_ffn_conv, meta_tokens, norm_mix, norm_ffn, norm_final, gdn_w_in, gdn_conv_w, gdn_A_log, gdn_dt_bias, gdn_norm_w, gdn_w_out, rwkv_mix, rwkv_wr, rwkv_wk, rwkv_wv, rwkv_wo, rwkv_w0, rwkv_w1, rwkv_w2, rwkv_a0, rwkv_a1, rwkv_a2, rwkv_g1, rwkv_g2, rwkv_k_k, rwkv_k_a, rwkv_r_k, rwkv_ln_w, rwkv_ln_b, rwkv_v0, rwkv_v1, rwkv_v2, ffn_w_up, ffn_conv_w, ffn_w_down):
    n_main = gdn_w_in.shape[-1] - 2 * GDN_HEADS
    w_ab = jnp.pad(gdn_w_in[..., n_main:], ((0, 0), (0, 0), (0, 128 - 2 * GDN_HEADS)))
    bf = lambda w: w.astype(BF16)
    p = dict(norm_mix=norm_mix, norm_ffn=norm_ffn, norm_final=norm_final,
             gdn_w_main=bf(gdn_w_in[..., :n_main]), gdn_w_ab=bf(w_ab),
             gdn_conv_w=gdn_conv_w, gdn_A_log=gdn_A_log, gdn_dt_bias=gdn_dt_bias,
             gdn_norm_w=gdn_norm_w, gdn_w_out=bf(gdn_w_out), rwkv_mix=rwkv_mix, rwkv_wr=bf(rwkv_wr),
             rwkv_wk=bf(rwkv_wk), rwkv_wv=bf(rwkv_wv), rwkv_wo=bf(rwkv_wo), rwkv_w0=rwkv_w0,
             rwkv_w1=bf(rwkv_w1), rwkv_w2=bf(rwkv_w2), rwkv_a0=rwkv_a0, rwkv_a1=bf(rwkv_a1),
             rwkv_a2=bf(rwkv_a2), rwkv_g1=bf(rwkv_g1), rwkv_g2=bf(rwkv_g2), rwkv_k_k=rwkv_k_k,
             rwkv_k_a=rwkv_k_a, rwkv_r_k=rwkv_r_k.reshape(rwkv_r_k.shape[0], 1, 1, -1, RWKV_N),
             rwkv_ln_w=rwkv_ln_w, rwkv_ln_b=rwkv_ln_b, rwkv_v0=rwkv_v0, rwkv_v1=bf(rwkv_v1),
             rwkv_v2=bf(rwkv_v2), ffn_w_up=bf(ffn_w_up), ffn_conv_w=ffn_conv_w, ffn_w_down=bf(ffn_w_down))

    B = x_prompt.shape[0]
    D = x_prompt.shape[-1]
    meta = jnp.broadcast_to(meta_tokens[None], (B, N_META, D))
    xp = jnp.concatenate([meta, x_prompt], axis=1)
    Lp = xp.shape[1]
    zeros_like_batch = lambda s: jnp.zeros((s.shape[0], B) + s.shape[2:], F32)
    yp, gS_p, gC_p, rS_p, rSh_p, fC_p = _trunk(
        _pad_tokens(xp, PROMPT_CHUNK), zeros_like_batch(state_gdn_S), zeros_like_batch(state_gdn_conv),
        zeros_like_batch(state_rwkv_S), zeros_like_batch(state_rwkv_shift), zeros_like_batch(state_ffn_conv),
        p, Lp, PROMPT_CHUNK)
    y_prompt = yp[:, N_META:Lp]

    Ls = x_sample.shape[1]
    ys, gS_s, gC_s, rS_s, rSh_s, fC_s = _trunk(
        _pad_tokens(x_sample, SAMPLE_CHUNK), state_gdn_S, state_gdn_conv, state_rwkv_S,
        state_rwkv_shift, state_ffn_conv, p, Ls, SAMPLE_CHUNK)
    y_sample = ys[:, :Ls]
    return (y_prompt, y_sample, gS_p, gC_p, rS_p, rSh_p, fC_p, gS_s, gC_s, rS_s, rSh_s, fC_s)
```

```python
import functools

import jax
import jax.numpy as jnp
from jax import lax
from jax.experimental import pallas as pl
from jax.experimental.pallas import tpu as pltpu

F32 = jnp.float32
BF16 = jnp.bfloat16

N_META = 16
GDN_HEADS = 8
GDN_DK = 128
GDN_DV = 128
RWKV_N = 64
RMS_EPS = 1e-6
L2_EPS = 1e-6
GN_EPS = 64e-5

PROMPT_CHUNK = 64
SAMPLE_CHUNK = 8
VMEM_LIMIT_BYTES = 48 * 1024 * 1024


def _pick_tile(n, cands):
    for c in cands:
        if n % c == 0:
            return c
    return n


def _dot(a, b):
    return jnp.dot(a.astype(BF16), b.astype(BF16), preferred_element_type=F32)


def _dot_nt(a, b):
    return lax.dot_general(a.astype(BF16), b.astype(BF16), (((1,), (1,)), ((), ())),
                           preferred_element_type=F32)


def _dot_tn(a, b):
    return lax.dot_general(a.astype(BF16), b.astype(BF16), (((0,), (0,)), ((), ())),
                           preferred_element_type=F32)


def _split2(x):
    hi = x.astype(BF16)
    return hi, (x - hi.astype(F32)).astype(BF16)


def _dot3(a, b):
    ah, al = _split2(a)
    bh, bl = _split2(b)
    d = lambda x, y: jnp.dot(x, y, preferred_element_type=F32)
    return d(ah, bh) + (d(ah, bl) + d(al, bh))


def _cumsum_rows(incl_bf16, x):
    hi = x.astype(BF16)
    r1 = x - hi.astype(F32)
    mid = r1.astype(BF16)
    lo = (r1 - mid.astype(F32)).astype(BF16)
    dot = lambda p: jnp.dot(incl_bf16, p, preferred_element_type=F32)
    return dot(hi) + dot(mid) + dot(lo)


def _gdn_kernel(q_ref, k_ref, v_ref, gcol_ref, bcol_ref, grow_ref, s0_ref,
                o_ref, sout_ref, s_scr, *, chunk, heads):
    c = pl.program_id(1)

    @pl.when(c == 0)
    def _():
        s_scr[...] = s0_ref[0]

    ri = lax.broadcasted_iota(jnp.int32, (chunk, chunk), 0)
    ci = lax.broadcasted_iota(jnp.int32, (chunk, chunk), 1)
    causal = ri >= ci
    strict = ri > ci
    hs = range(heads)
    sl = [slice(h * GDN_DK, (h + 1) * GDN_DK) for h in hs]
    q = [q_ref[0, :, sl[h]] for h in hs]
    k = [k_ref[0, :, sl[h]] for h in hs]
    v = [v_ref[0, :, sl[h]] for h in hs]
    g_col = [gcol_ref[0, :, h:h + 1] for h in hs]
    beta = [bcol_ref[0, :, h:h + 1] for h in hs]
    g_row = [grow_ref[0, 0, h:h + 1, :] for h in hs]
    G_col = [jnp.sum(jnp.where(causal, g_row[h], 0.0), axis=1, keepdims=True) for h in hs]
    G_row = [jnp.sum(jnp.where(ri <= ci, g_col[h], 0.0), axis=0, keepdims=True) for h in hs]
    g_last = [jnp.sum(g_row[h], axis=1, keepdims=True) for h in hs]
    decay = [jnp.where(causal, jnp.exp(jnp.minimum(G_col[h] - G_row[h], 0.0)), 0.0) for h in hs]
    eG = [jnp.exp(G_col[h]) for h in hs]
    k_bf = [k[h].astype(BF16) for h in hs]
    kb = [k[h] * beta[h] for h in hs]
    S = [s_scr[h] for h in hs]
    S_bf = [S[h].astype(BF16) for h in hs]
    n_strict = [jnp.where(strict, _dot_nt(kb[h], k_bf[h]) * decay[h], 0.0) for h in hs]
    qk = [(_dot_nt(q[h], k_bf[h]) * decay[h]).astype(BF16) for h in hs]
    qS = [_dot(q[h] * eG[h], S_bf[h]) for h in hs]
    x = [jnp.concatenate([v[h] * beta[h], kb[h] * eG[h]], axis=1) for h in hs]
    p = n_strict
    x = [x[h] - _dot3(p[h], x[h]) for h in hs]
    span = 2
    while span < chunk:
        p = [_dot3(p[h], p[h]) for h in hs]
        x = [x[h] + _dot3(p[h], x[h]) for h in hs]
        span *= 2
    v_new = [x[h][:, :GDN_DV] - _dot(x[h][:, GDN_DV:], S_bf[h]) for h in hs]
    v_new_bf = [v_new[h].astype(BF16) for h in hs]
    for h in hs:
        o_ref[0, :, sl[h]] = qS[h] + _dot(qk[h], v_new_bf[h])
    for h in hs:
        k_dec = k[h] * jnp.exp(g_last[h] - G_col[h])
        s_scr[h] = S[h] * jnp.exp(g_last[h]) + _dot_tn(k_dec, v_new_bf[h])

    @pl.when(c == pl.num_programs(1) - 1)
    def _():
        sout_ref[0] = s_scr[...]


def gdn_recurrence(qkv, g, beta, S0, chunk):
    B, L, _ = qkv.shape
    H = GDN_HEADS
    n = L // chunk
    g_row = jnp.transpose(g, (0, 2, 1)).reshape(B, H, n, chunk).transpose(0, 2, 1, 3)
    blk = lambda part: pl.BlockSpec((1, chunk, H * GDN_DK), lambda b, c, part=part: (b, c, part))
    col = pl.BlockSpec((1, chunk, H), lambda b, c: (b, c, 0))
    st = pl.BlockSpec((1, H, GDN_DK, GDN_DV), lambda b, c: (b, 0, 0, 0))
    o, S = pl.pallas_call(
        functools.partial(_gdn_kernel, chunk=chunk, heads=H),
        grid=(B, n),
        in_specs=[blk(0), blk(1), blk(2), col, col,
                  pl.BlockSpec((1, 1, H, chunk), lambda b, c: (b, c, 0, 0)), st],
        out_specs=[blk(0), st],
        out_shape=[jax.ShapeDtypeStruct((B, L, H * GDN_DV), F32),
                   jax.ShapeDtypeStruct(S0.shape, F32)],
        scratch_shapes=[pltpu.VMEM((H, GDN_DK, GDN_DV), F32)],
        compiler_params=pltpu.CompilerParams(
            dimension_semantics=("parallel", "arbitrary"),
            vmem_limit_bytes=VMEM_LIMIT_BYTES),
        name="gdn_recurrence",
    )(qkv, qkv, qkv, g, beta, g_row, S0)
    return o, S


def _rwkv_kernel(r_ref, lw_ref, k_ref, v_ref, kk_ref, b_ref, s0_ref,
                 o_ref, sout_ref, s_scr, *, chunk, heads):
    c = pl.program_id(1)

    @pl.when(c == 0)
    def _():
        s_scr[...] = s0_ref[0]

    ri = lax.broadcasted_iota(jnp.int32, (chunk, chunk), 0)
    ci = lax.broadcasted_iota(jnp.int32, (chunk, chunk), 1)
    incl = (ri >= ci).astype(BF16)
    ri2 = lax.broadcasted_iota(jnp.int32, (2 * chunk, 2 * chunk), 0)
    ci2 = lax.broadcasted_iota(jnp.int32, (2 * chunk, 2 * chunk), 1)
    cj = jnp.where(ci2 >= chunk, ci2 - chunk, ci2)
    keep = ((ri2 < chunk) & (ri2 > cj)) | ((ri2 >= chunk) & ((ri2 - chunk) >= cj))

    lw = lw_ref[0]
    LP = _cumsum_rows(incl, lw)
    LP_last = LP[chunk - 1:chunk, :]
    P = jnp.exp(LP)
    P_inv = jnp.exp(-LP)
    dec = jnp.exp(LP_last - LP)
    P_last = jnp.exp(LP_last)
    kk = kk_ref[0]
    b = b_ref[0]
    k = k_ref[0]
    rows_k = (kk * jnp.exp(LP - lw)).astype(BF16)
    rows_r = (r_ref[0] * P).astype(BF16)
    cols_k = (k * P_inv).astype(BF16)
    cols_b = (b * P_inv).astype(BF16)
    k_dec = (k * dec).astype(BF16)
    b_dec = (b * dec).astype(BF16)
    v_bf = v_ref[0].astype(BF16)

    hs = range(heads)
    sl = [slice(h * RWKV_N, (h + 1) * RWKV_N) for h in hs]
    rows = [jnp.concatenate([rows_k[:, sl[h]], rows_r[:, sl[h]]], axis=0) for h in hs]
    cols = [jnp.concatenate([cols_k[:, sl[h]], cols_b[:, sl[h]]], axis=0) for h in hs]
    S = [s_scr[h] for h in hs]
    A = [jnp.where(keep, _dot_nt(rows[h], cols[h]), 0.0).astype(BF16) for h in hs]
    base = [_dot_nt(rows[h], S[h]) for h in hs]
    vh = [v_bf[:, sl[h]] for h in hs]
    x = [base[h][:chunk] + _dot(A[h][:chunk, :chunk], vh[h]) for h in hs]
    p = [A[h][:chunk, chunk:] for h in hs]
    x = [x[h] - _dot(p[h], x[h]) for h in hs]
    span = 2
    while span < chunk:
        p = [_dot(p[h], p[h]).astype(BF16) for h in hs]
        x = [x[h] + _dot(p[h], x[h]) for h in hs]
        span *= 2
    u = [x[h].astype(BF16) for h in hs]
    for h in hs:
        o_ref[0, :, sl[h]] = (base[h][chunk:] + _dot(A[h][chunk:, :chunk], vh[h])
                              - _dot(A[h][chunk:, chunk:], u[h]))
    for h in hs:
        s_scr[h] = (S[h] * P_last[:, sl[h]] + _dot_tn(vh[h], k_dec[:, sl[h]])
                    - _dot_tn(u[h], b_dec[:, sl[h]]))

    @pl.when(c == pl.num_programs(1) - 1)
    def _():
        sout_ref[0] = s_scr[...]


def rwkv_recurrence(r, lw, k, v, kk, b, S0, chunk):
    B, L, D = r.shape
    n = L // chunk
    heads = D // RWKV_N
    blk = pl.BlockSpec((1, chunk, D), lambda bb, c: (bb, c, 0))
    st = pl.BlockSpec((1, heads, RWKV_N, RWKV_N), lambda bb, c: (bb, 0, 0, 0))
    o, S = pl.pallas_call(
        functools.partial(_rwkv_kernel, chunk=chunk, heads=heads),
        grid=(B, n),
        in_specs=[blk] * 6 + [st],
        out_specs=[blk, st],
        out_shape=[jax.ShapeDtypeStruct((B, L, D), F32),
                   jax.ShapeDtypeStruct(S0.shape, F32)],
        scratch_shapes=[pltpu.VMEM((heads, RWKV_N, RWKV_N), F32)],
        compiler_params=pltpu.CompilerParams(
            dimension_semantics=("parallel", "arbitrary"),
            vmem_limit_bytes=VMEM_LIMIT_BYTES),
        name="rwkv_recurrence",
    )(r, lw, k, v, kk, b, S0)
    return o, S


ROW_TILES = (528, 512, 256, 128, 64, 32, 16, 8)
LAYER_VMEM_LIMIT_BYTES = 58 * 1024 * 1024


def _rms(x, eps):
    return x * lax.rsqrt(jnp.mean(x * x, axis=-1, keepdims=True) + eps)


def _softplus(x):
    return jnp.maximum(x, 0.0) + jnp.log1p(jnp.exp(-jnp.abs(x)))


def _sigmoid(x):
    return 1.0 / (1.0 + jnp.exp(-x))


def _exact_dot(x, const_bf16):
    hi = x.astype(BF16)
    r1 = x - hi.astype(F32)
    mid = r1.astype(BF16)
    lo = (r1 - mid.astype(F32)).astype(BF16)
    dot = lambda p: jnp.dot(p, const_bf16, preferred_element_type=F32)
    return dot(hi) + dot(mid) + dot(lo)


def _head_indicator(d, n):
    shift = n.bit_length() - 1
    assert 1 << shift == n
    ch = lax.shift_right_logical(lax.broadcasted_iota(jnp.int32, (d, 128), 0), shift)
    hd = lax.broadcasted_iota(jnp.int32, (d, 128), 1)
    cht = lax.shift_right_logical(lax.broadcasted_iota(jnp.int32, (128, d), 1), shift)
    hdt = lax.broadcasted_iota(jnp.int32, (128, d), 0)
    return (ch == hd).astype(BF16), (cht == hdt).astype(BF16)


def _shifted_rows(buf, carry_rows, cur, shift, taps, tm):
    prev = carry_rows.shape[0]
    pad = buf.shape[0] - tm
    buf[pad - prev:pad, :] = carry_rows
    buf[pad:, :] = cur
    return [buf[pad - j * shift:pad - j * shift + tm, :] for j in range(1, taps + 1)]


def _row_grid_call(kernel_fn, name, G, R, tm, in_specs, out_specs, out_shape, scratch_shapes):
    return pl.pallas_call(
        kernel_fn, grid=(G, R // tm), in_specs=in_specs, out_specs=out_specs, out_shape=out_shape,
        scratch_shapes=scratch_shapes,
        compiler_params=pltpu.CompilerParams(dimension_semantics=("parallel", "arbitrary"),
                                             vmem_limit_bytes=LAYER_VMEM_LIMIT_BYTES),
        name=name)


def _tile_spec(tm, c):
    return pl.BlockSpec((1, tm, c), lambda g, t: (g, t, 0))


def _group_spec(rows, c):
    return pl.BlockSpec((1, rows, c), lambda g, t: (g, 0, 0))


def _layer_spec(layer, *blk):
    return pl.BlockSpec((1,) + blk, lambda g, t: (layer,) + (0,) * len(blk))


def _state_row(R, tm, valid_rows, prev):
    row = valid_rows - prev - (R // tm - 1) * tm
    assert 0 <= row and prev <= tm, "the new conv/shift state must come from the last row tile"
    return row


FFN_COL_CHUNK = 256


def _ffn_kernel(x_ref, nw_ref, wup_ref, cw_ref, wdn_ref, cbuf_ref, fnw_ref,
                o_ref, cout_ref, gbuf, carry, *, tm, shift, d_ff, state_row, final_norm):
    t = pl.program_id(1)
    x = x_ref[0]
    hn = (_rms(x, RMS_EPS) * nw_ref[0]).astype(BF16)
    taps = cw_ref.shape[1] - 1
    prev = taps * shift

    @pl.when(t == 0)
    def _():
        carry[...] = cbuf_ref[0]

    acc = jnp.zeros(x.shape, F32)
    for c in range(d_ff // FFN_COL_CHUNK):
        cols = slice(c * FFN_COL_CHUNK, (c + 1) * FFN_COL_CHUNK)
        ucols = slice(d_ff + c * FFN_COL_CHUNK, d_ff + (c + 1) * FFN_COL_CHUNK)
        g = jnp.dot(hn, wup_ref[0, :, cols], preferred_element_type=F32)
        u = jnp.dot(hn, wup_ref[0, :, ucols], preferred_element_type=F32)
        delayed = _shifted_rows(gbuf.at[c % 2], carry[:, cols], g, shift, taps, tm)
        w = cw_ref[0, :, cols]
        gc = g * w[taps:taps + 1]
        for j in range(1, taps + 1):
            gc = gc + delayed[j - 1] * w[taps - j:taps - j + 1]
        act = (gc * _sigmoid(gc) * u).astype(BF16)
        acc = acc + jnp.dot(act, wdn_ref[0, cols, :], preferred_element_type=F32)
        carry[:, cols] = g[tm - prev:, :]
        cout_ref[0, :, cols] = g[state_row:state_row + prev, :]
    y = x + acc
    if final_norm:
        y = _rms(y, RMS_EPS) * fnw_ref[...]
    o_ref[0] = y


def ffn_layer(x, conv_state, p, layer, lay, final_norm):
    G, R, D = x.shape
    d_ff = p['ffn_conv_w'].shape[-1]
    taps = p['ffn_conv_w'].shape[1] - 1
    shift = lay['shift']
    prev = taps * shift
    tm = _pick_tile(R, ROW_TILES)
    pad = -(-prev // 8) * 8
    kern = functools.partial(_ffn_kernel, tm=tm, shift=shift, d_ff=d_ff,
                             state_row=_state_row(R, tm, lay['valid'], prev), final_norm=final_norm)
    return _row_grid_call(
        kern, "ffn_layer", G, R, tm,
        in_specs=[_tile_spec(tm, D), _layer_spec(layer, 1, D), _layer_spec(layer, D, 2 * d_ff),
                  _layer_spec(layer, taps + 1, d_ff), _layer_spec(layer, d_ff, D), _group_spec(prev, d_ff),
                  pl.BlockSpec((1, D), lambda g, t: (0, 0))],
        out_specs=[_tile_spec(tm, D), _group_spec(prev, d_ff)],
        out_shape=[jax.ShapeDtypeStruct((G, R, D), F32), jax.ShapeDtypeStruct((G, prev, d_ff), F32)],
        scratch_shapes=[pltpu.VMEM((2, pad + tm, FFN_COL_CHUNK), F32), pltpu.VMEM((prev, d_ff), F32)],
    )(x, p['norm_ffn3'], p['ffn_w_up'], p['ffn_conv_w'], p['ffn_w_down'], conv_state,
      p['norm_final'].reshape(1, D))


GDN_COL_CHUNK = 256


def _gdn_pre_kernel(x_ref, nw_ref, w_ref, wab_ref, cw_ref, al_ref, dt_ref, cbuf_ref,
                    qkv_ref, z_ref, g_ref, beta_ref, cout_ref, gbuf, carry,
                    *, tm, shift, conv_dim, state_row, valid_rows, heads):
    t = pl.program_id(1)
    hn = (_rms(x_ref[0], RMS_EPS) * nw_ref[0]).astype(BF16)
    taps = cw_ref.shape[1] - 1
    prev = taps * shift

    @pl.when(t == 0)
    def _():
        carry[...] = cbuf_ref[0]

    qk_dim = heads * GDN_DK
    for c in range(conv_dim // GDN_COL_CHUNK):
        cols = slice(c * GDN_COL_CHUNK, (c + 1) * GDN_COL_CHUNK)
        pre = jnp.dot(hn, w_ref[0, :, cols], preferred_element_type=F32)
        delayed = _shifted_rows(gbuf.at[c % 2], carry[:, cols], pre, shift, taps, tm)
        w = cw_ref[0, :, cols]
        acc = pre * w[taps:taps + 1]
        for j in range(1, taps + 1):
            acc = acc + delayed[j - 1] * w[taps - j:taps - j + 1]
        act = acc * _sigmoid(acc)
        carry[:, cols] = pre[tm - prev:, :]
        cout_ref[0, :, cols] = pre[state_row:state_row + prev, :]
        if c * GDN_COL_CHUNK < 2 * qk_dim:
            scale = GDN_DK ** -0.5 if c * GDN_COL_CHUNK < qk_dim else 1.0
            for h in range(GDN_COL_CHUNK // GDN_DK):
                a_h = act[:, h * GDN_DK:(h + 1) * GDN_DK]
                n_h = a_h * lax.rsqrt(jnp.sum(a_h * a_h, axis=-1, keepdims=True) + L2_EPS)
                lo = c * GDN_COL_CHUNK + h * GDN_DK
                qkv_ref[0, :, lo:lo + GDN_DK] = n_h * scale if scale != 1.0 else n_h
        else:
            qkv_ref[0, :, cols] = act
    z_dim = z_ref.shape[-1]
    for c in range(z_dim // GDN_COL_CHUNK):
        cols = slice(c * GDN_COL_CHUNK, (c + 1) * GDN_COL_CHUNK)
        wcols = slice(conv_dim + c * GDN_COL_CHUNK, conv_dim + (c + 1) * GDN_COL_CHUNK)
        z = jnp.dot(hn, w_ref[0, :, wcols], preferred_element_type=F32)
        z_ref[0, :, cols] = z * _sigmoid(z)
    ab = jnp.dot(hn, wab_ref[0], preferred_element_type=F32)
    row = t * tm + lax.broadcasted_iota(jnp.int32, (tm, 1), 0)
    m = (row < valid_rows).astype(F32)
    g_ref[0] = -jnp.exp(al_ref[0]) * _softplus(ab[:, :heads] + dt_ref[0]) * m
    beta_ref[0] = _sigmoid(ab[:, heads:2 * heads]) * m


def gdn_pre(x, conv_state, p, i, layer, lay):
    G, R, D = x.shape
    H = GDN_HEADS
    conv_dim = p['gdn_conv_w'].shape[-1]
    z_dim = p['gdn_w_main'].shape[-1] - conv_dim
    taps = p['gdn_conv_w'].shape[1] - 1
    shift = lay['shift']
    prev = taps * shift
    tm = _pick_tile(R, ROW_TILES)
    pad = -(-prev // 8) * 8
    kern = functools.partial(_gdn_pre_kernel, tm=tm, shift=shift, conv_dim=conv_dim,
                             state_row=_state_row(R, tm, lay['valid'], prev), valid_rows=lay['valid'], heads=H)
    return _row_grid_call(
        kern, "gdn_pre", G, R, tm,
        in_specs=[_tile_spec(tm, D), _layer_spec(i, 1, D), _layer_spec(layer, D, conv_dim + z_dim),
                  _layer_spec(layer, D, 128), _layer_spec(layer, taps + 1, conv_dim),
                  _layer_spec(layer, 1, H), _layer_spec(layer, 1, H), _group_spec(prev, conv_dim)],
        out_specs=[_tile_spec(tm, conv_dim), _tile_spec(tm, z_dim), _tile_spec(tm, H), _tile_spec(tm, H),
                   _group_spec(prev, conv_dim)],
        out_shape=[jax.ShapeDtypeStruct((G, R, conv_dim), F32), jax.ShapeDtypeStruct((G, R, z_dim), F32),
                   jax.ShapeDtypeStruct((G, R, H), F32), jax.ShapeDtypeStruct((G, R, H), F32),
                   jax.ShapeDtypeStruct((G, prev, conv_dim), F32)],
        scratch_shapes=[pltpu.VMEM((2, pad + tm, GDN_COL_CHUNK), F32), pltpu.VMEM((prev, conv_dim), F32)],
    )(x, p['norm_mix3'], p['gdn_w_main'], p['gdn_w_ab'], p['gdn_conv_w'], p['gdn_A_log3'], p['gdn_dt_bias3'], conv_state)


def _gdn_post_kernel(x_ref, o_ref, z_ref, nw_ref, w_ref, y_ref, *, heads):
    o = o_ref[0]
    parts = []
    for h in range(heads):
        o_h = o[:, h * GDN_DV:(h + 1) * GDN_DV]
        parts.append(_rms(o_h, RMS_EPS) * nw_ref[0])
    gated = (jnp.concatenate(parts, axis=1) * z_ref[0]).astype(BF16)
    y_ref[0] = x_ref[0] + jnp.dot(gated, w_ref[0], preferred_element_type=F32)


def gdn_post(x, o, z_act, p, layer):
    G, R, D = x.shape
    v_dim = o.shape[-1]
    tm = _pick_tile(R, ROW_TILES)
    return _row_grid_call(
        functools.partial(_gdn_post_kernel, heads=GDN_HEADS), "gdn_post", G, R, tm,
        in_specs=[_tile_spec(tm, D), _tile_spec(tm, v_dim), _tile_spec(tm, v_dim),
                  _layer_spec(layer, 1, GDN_DV), _layer_spec(layer, v_dim, D)],
        out_specs=_tile_spec(tm, D),
        out_shape=jax.ShapeDtypeStruct((G, R, D), F32),
        scratch_shapes=[],
    )(x, o, z_act, p['gdn_norm_w3'], p['gdn_w_out'])


RWKV_ROW_TILES = (352, 256, 128, 64, 32, 16, 8)


def _rwkv_pre_kernel(*refs, tm, shift, state_row, valid_rows, has_vres):
    (x_ref, nw_ref, mix_ref, wr_ref, wk_ref, wv_ref, w1_ref, w2_ref, a1_ref, a2_ref, g1_ref, g2_ref,
     w0_ref, a0_ref, kk_ref, ka_ref, rk_ref, sh_ref) = refs[:18]
    refs = refs[18:]
    if has_vres:
        v1_ref, v2_ref, v0_ref, vf_ref = refs[:4]
        refs = refs[4:]
    (r_ref, lw_ref, kf_ref, v_ref, kkn_ref, b_ref, gate_ref, bonus_ref, shout_ref, hbuf, carry) = refs
    t = pl.program_id(1)

    @pl.when(t == 0)
    def _():
        carry[...] = sh_ref[0]

    hn = _rms(x_ref[0], RMS_EPS) * nw_ref[0]
    prev = _shifted_rows(hbuf, carry[...], hn, shift, 1, tm)[0]
    carry[...] = hn[tm - shift:, :]
    shout_ref[0] = hn[state_row:state_row + shift, :]
    xx = prev - hn
    mixed = lambda i: (hn + xx * mix_ref[0, i:i + 1, :]).astype(BF16)
    dot = lambda a, w_ref: jnp.dot(a, w_ref[0], preferred_element_type=F32)
    xr, xw, xk, xv, xa, xg = (mixed(i) for i in range(6))
    r = dot(xr, wr_ref)
    k = dot(xk, wk_ref)
    v = dot(xv, wv_ref)
    w_lora = dot(jnp.tanh(dot(xw, w1_ref)).astype(BF16), w2_ref)
    w_log = -_softplus(-(w0_ref[0] + w_lora)) - 0.5
    if has_vres:
        v_lora = dot(dot(xv, v1_ref).astype(BF16), v2_ref)
        v = v + (vf_ref[0] - v) * _sigmoid(v0_ref[0] + v_lora)
    a = _sigmoid(a0_ref[0] + dot(dot(xa, a1_ref).astype(BF16), a2_ref))
    gate_ref[0] = dot(_sigmoid(dot(xg, g1_ref)).astype(BF16), g2_ref)

    D = hn.shape[-1]
    seg, seg_t = _head_indicator(D, RWKV_N)
    kkk = k * kk_ref[0]
    rs = lax.rsqrt(_exact_dot(kkk * kkk, seg) + L2_EPS)
    kkn = kkk * _exact_dot(rs, seg_t)
    kf = k * (1.0 + (a - 1.0) * ka_ref[0])
    bonus_ref[0] = _exact_dot(r * kf * rk_ref[0], seg)

    row = t * tm + lax.broadcasted_iota(jnp.int32, (tm, 1), 0)
    m = (row < valid_rows).astype(F32)
    kkn = kkn * m
    r_ref[0] = r
    lw_ref[0] = -jnp.exp(w_log) * m
    kf_ref[0] = kf * m
    v_ref[0] = v * m
    kkn_ref[0] = kkn
    b_ref[0] = kkn * a


def rwkv_pre(x, shift_state, v_first, p, i, layer, lay):
    G, R, D = x.shape
    s = lay['shift']
    tm = _pick_tile(R, RWKV_ROW_TILES)
    pad = -(-s // 8) * 8
    has_vres = layer > 0
    L = lambda *blk: _layer_spec(layer, *blk)
    lora = lambda name: p[name].shape[-1]
    in_specs = [_tile_spec(tm, D), _layer_spec(i, 1, D), L(6, D), L(D, D), L(D, D), L(D, D),
                L(D, lora('rwkv_w1')), L(lora('rwkv_w1'), D), L(D, lora('rwkv_a1')), L(lora('rwkv_a1'), D),
                L(D, lora('rwkv_g1')), L(lora('rwkv_g1'), D), L(1, D), L(1, D), L(1, D), L(1, D), L(1, D),
                _group_spec(s, D)]
    args = [x, p['norm_mix3'], p['rwkv_mix'], p['rwkv_wr'], p['rwkv_wk'], p['rwkv_wv'], p['rwkv_w1'],
            p['rwkv_w2'], p['rwkv_a1'], p['rwkv_a2'], p['rwkv_g1'], p['rwkv_g2'], p['rwkv_w03'], p['rwkv_a03'],
            p['rwkv_k_k3'], p['rwkv_k_a3'], p['rwkv_r_k3'], shift_state]
    if has_vres:
        V = lambda *blk: _layer_spec(layer - 1, *blk)
        in_specs += [V(D, lora('rwkv_v1')), V(lora('rwkv_v1'), D), V(1, D), _tile_spec(tm, D)]
        args += [p['rwkv_v1'], p['rwkv_v2'], p['rwkv_v03'], v_first]
    act = jax.ShapeDtypeStruct((G, R, D), F32)
    kern = functools.partial(_rwkv_pre_kernel, tm=tm, shift=s, state_row=_state_row(R, tm, lay['valid'], s),
                             valid_rows=lay['valid'], has_vres=has_vres)
    return _row_grid_call(
        kern, "rwkv_pre", G, R, tm, in_specs=in_specs,
        out_specs=[_tile_spec(tm, D)] * 7 + [_tile_spec(tm, 128), _group_spec(s, D)],
        out_shape=[act] * 7 + [jax.ShapeDtypeStruct((G, R, 128), F32), jax.ShapeDtypeStruct((G, s, D), F32)],
        scratch_shapes=[pltpu.VMEM((pad + tm, D), F32), pltpu.VMEM((s, D), F32)],
    )(*args)


def _rwkv_post_kernel(x_ref, y_ref, v_ref, gate_ref, bonus_ref, lnw_ref, lnb_ref, wo_ref, o_ref):
    y = y_ref[0]
    D = y.shape[-1]
    seg, seg_t = _head_indicator(D, RWKV_N)
    inv_n = 1.0 / RWKV_N
    mu = _exact_dot(_exact_dot(y, seg) * inv_n, seg_t)
    d = y - mu
    rs = lax.rsqrt(_exact_dot(d * d, seg) * inv_n + GN_EPS)
    yn = d * _exact_dot(rs, seg_t) * lnw_ref[0] + lnb_ref[0]
    yn = yn + _exact_dot(bonus_ref[0], seg_t) * v_ref[0]
    o_ref[0] = x_ref[0] + jnp.dot((yn * gate_ref[0]).astype(BF16), wo_ref[0], preferred_element_type=F32)


def rwkv_post(x, y, v, gate, bonus, p, layer):
    G, R, D = x.shape
    tm = _pick_tile(R, RWKV_ROW_TILES)
    L = lambda *blk: _layer_spec(layer, *blk)
    return _row_grid_call(
        _rwkv_post_kernel, "rwkv_post", G, R, tm,
        in_specs=[_tile_spec(tm, D)] * 4 + [_tile_spec(tm, 128), L(1, D), L(1, D), L(D, D)],
        out_specs=_tile_spec(tm, D),
        out_shape=jax.ShapeDtypeStruct((G, R, D), F32),
        scratch_shapes=[],
    )(x, y, v, gate, bonus, p['rwkv_ln_w3'], p['rwkv_ln_b3'], p['rwkv_wo'])


def _to_seq(a, lay):
    if lay['shift'] == 1:
        return a
    B = lay['shift']
    T = a.shape[1] // B
    a = a.reshape(T, B, a.shape[-1]).transpose(1, 0, 2)
    return jnp.pad(a, ((0, 0), (0, lay['chunk'] - T), (0, 0)))


def _from_seq(o, lay):
    if lay['shift'] == 1:
        return o
    B = lay['shift']
    T = lay['valid'] // B
    return o[:, :T].transpose(1, 0, 2).reshape(1, T * B, o.shape[-1])


def _gdn_mixer(x, conv_buf, S0, p, i, j, lay):
    qkv_n, z_act, g, beta, new_buf = gdn_pre(x, conv_buf, p, i, j, lay)
    o, S = gdn_recurrence(_to_seq(qkv_n, lay), _to_seq(g, lay), _to_seq(beta, lay), S0, lay['chunk'])
    return gdn_post(x, _from_seq(o, lay), z_act, p, j), new_buf, S


def _rwkv_mixer(x, shift, S0, v_first, p, i, j, lay):
    r, lw, kf, v, kk, b, gate, bonus, new_shift = rwkv_pre(x, shift, v_first, p, i, j, lay)
    seq = lambda t: _to_seq(t, lay)
    y, S = rwkv_recurrence(seq(r), seq(lw), seq(kf), seq(v), seq(kk), seq(b), S0, lay['chunk'])
    if j == 0:
        v_first = v
    return rwkv_post(x, _from_seq(y, lay), v, gate, bonus, p, j), new_shift, S, v_first


def _trunk(x, gdn_S, gdn_conv, rwkv_S, rwkv_shift, ffn_conv, p, lay):
    depth = p['depth']
    gS, gC, rS, rSh, fC = [], [], [], [], []
    v_first = None
    for i in range(depth):
        j = i // 2
        if i % 2 == 0:
            x, cb, S = _gdn_mixer(x, gdn_conv[j], gdn_S[j], p, i, j, lay)
            gS.append(S)
            gC.append(cb)
        else:
            x, sh, S, v_first = _rwkv_mixer(x, rwkv_shift[j], rwkv_S[j], v_first, p, i, j, lay)
            rS.append(S)
            rSh.append(sh)
        x, fb = ffn_layer(x, ffn_conv[i], p, i, lay, final_norm=(i == depth - 1))
        fC.append(fb)
    return x, jnp.stack(gS), jnp.stack(gC), jnp.stack(rS), jnp.stack(rSh), jnp.stack(fC)


def _pad_tokens(x, chunk):
    L = x.shape[1]
    Lp = -(-L // chunk) * chunk
    return jnp.pad(x, ((0, 0), (0, Lp - L), (0, 0)))


def _rows_state(s):
    N, B, W, C = s.shape
    return s.transpose(0, 2, 1, 3).reshape(N, 1, W * B, C)


def _seq_state(s, B):
    N, _, WB, C = s.shape
    return s.reshape(N, WB // B, B, C).transpose(0, 2, 1, 3)


def kernel(x_prompt, x_sample, state_gdn_S, state_gdn_conv, state_rwkv_S, state_rwkv_shift, state_ffn_conv, meta_tokens, norm_mix, norm_ffn, norm_final, gdn_w_in, gdn_conv_w, gdn_A_log, gdn_dt_bias, gdn_norm_w, gdn_w_out, rwkv_mix, rwkv_wr, rwkv_wk, rwkv_wv, rwkv_wo, rwkv_w0, rwkv_w1, rwkv_w2, rwkv_a0, rwkv_a1, rwkv_a2, rwkv_g1, rwkv_g2, rwkv_k_k, rwkv_k_a, rwkv_r_k, rwkv_ln_w, rwkv_ln_b, rwkv_v0, rwkv_v1, rwkv_v2, ffn_w_up, ffn_conv_w, ffn_w_down):
    n_main = gdn_w_in.shape[-1] - 2 * GDN_HEADS
    w_ab = jnp.pad(gdn_w_in[..., n_main:], ((0, 0), (0, 0), (0, 128 - 2 * GDN_HEADS)))
    bf = lambda w: w.astype(BF16)
    three = lambda a: a.reshape(a.shape[0], 1, a.shape[-1])
    p = dict(depth=norm_mix.shape[0], norm_mix3=three(norm_mix), norm_ffn3=three(norm_ffn), norm_final=norm_final,
             gdn_w_main=bf(gdn_w_in[..., :n_main]), gdn_w_ab=bf(w_ab), gdn_conv_w=gdn_conv_w,
             gdn_A_log3=three(gdn_A_log), gdn_dt_bias3=three(gdn_dt_bias), gdn_norm_w3=three(gdn_norm_w),
             gdn_w_out=bf(gdn_w_out), rwkv_mix=rwkv_mix, rwkv_wr=bf(rwkv_wr), rwkv_wk=bf(rwkv_wk),
             rwkv_wv=bf(rwkv_wv), rwkv_wo=bf(rwkv_wo), rwkv_w1=bf(rwkv_w1), rwkv_w2=bf(rwkv_w2),
             rwkv_a1=bf(rwkv_a1), rwkv_a2=bf(rwkv_a2), rwkv_g1=bf(rwkv_g1), rwkv_g2=bf(rwkv_g2),
             rwkv_v1=bf(rwkv_v1), rwkv_v2=bf(rwkv_v2), rwkv_w03=three(rwkv_w0), rwkv_a03=three(rwkv_a0),
             rwkv_v03=three(rwkv_v0), rwkv_k_k3=three(rwkv_k_k), rwkv_k_a3=three(rwkv_k_a),
             rwkv_r_k3=rwkv_r_k.reshape(rwkv_r_k.shape[0], 1, -1), rwkv_ln_w3=three(rwkv_ln_w),
             rwkv_ln_b3=three(rwkv_ln_b), ffn_w_up=bf(ffn_w_up), ffn_conv_w=ffn_conv_w, ffn_w_down=bf(ffn_w_down))

    B = x_prompt.shape[0]
    D = x_prompt.shape[-1]
    meta = jnp.broadcast_to(meta_tokens[None], (B, N_META, D))
    xp = jnp.concatenate([meta, x_prompt], axis=1)
    Lp = xp.shape[1]
    zeros_like_batch = lambda s: jnp.zeros((s.shape[0], B) + s.shape[2:], F32)
    yp, gS_p, gC_p, rS_p, rSh_p, fC_p = _trunk(
        _pad_tokens(xp, PROMPT_CHUNK), zeros_like_batch(state_gdn_S), zeros_like_batch(state_gdn_conv),
        zeros_like_batch(state_rwkv_S), zeros_like_batch(state_rwkv_shift)[:, :, None],
        zeros_like_batch(state_ffn_conv), p, dict(shift=1, valid=Lp, chunk=PROMPT_CHUNK))
    y_prompt = yp[:, N_META:Lp]
    rSh_p = rSh_p[:, :, 0]

    Bs, Ls, _ = x_sample.shape
    xs = x_sample.transpose(1, 0, 2).reshape(1, Ls * Bs, D)
    ys, gS_s, gC_s, rS_s, rSh_s, fC_s = _trunk(
        xs, state_gdn_S, _rows_state(state_gdn_conv), state_rwkv_S, state_rwkv_shift[:, None],
        _rows_state(state_ffn_conv), p, dict(shift=Bs, valid=Ls * Bs, chunk=SAMPLE_CHUNK))
    y_sample = ys.reshape(Ls, Bs, D).transpose(1, 0, 2)
    gC_s = _seq_state(gC_s, Bs)
    fC_s = _seq_state(fC_s, Bs)
    rSh_s = rSh_s[:, 0]
    return (y_prompt, y_sample, gS_p, gC_p, rS_p, rSh_p, fC_p, gS_s, gC_s, rS_s, rSh_s, fC_s)
```

```python
import functools

import jax
import jax.numpy as jnp
from jax import lax
from jax.experimental import pallas as pl
from jax.experimental.pallas import tpu as pltpu

F32 = jnp.float32
BF16 = jnp.bfloat16

N_META = 16
GDN_HEADS = 8
GDN_DK = 128
GDN_DV = 128
RWKV_N = 64
RMS_EPS = 1e-6
L2_EPS = 1e-6
GN_EPS = 64e-5

PROMPT_CHUNK = 64
SAMPLE_CHUNK = 8
SAMPLE_SEQS_PER_STEP = 4
SOLVE_SPLIT_SPAN = 8
VMEM_LIMIT_BYTES = 48 * 1024 * 1024


def _pick_tile(n, cands):
    for c in cands:
        if n % c == 0:
            return c
    return n


def _dot(a, b):
    return jnp.dot(a.astype(BF16), b.astype(BF16), preferred_element_type=F32)


def _dot_nt(a, b):
    return lax.dot_general(a.astype(BF16), b.astype(BF16), (((1,), (1,)), ((), ())),
                           preferred_element_type=F32)


def _dot_tn(a, b):
    return lax.dot_general(a.astype(BF16), b.astype(BF16), (((0,), (0,)), ((), ())),
                           preferred_element_type=F32)


def _split2(x):
    hi = x.astype(BF16)
    return hi, (x - hi.astype(F32)).astype(BF16)


def _dot3(a, b):
    ah, al = _split2(a)
    bh, bl = _split2(b)
    d = lambda x, y: jnp.dot(x, y, preferred_element_type=F32)
    return d(ah, bh) + (d(ah, bl) + d(al, bh))


def _cumsum_rows(incl_bf16, x):
    hi = x.astype(BF16)
    r1 = x - hi.astype(F32)
    mid = r1.astype(BF16)
    lo = (r1 - mid.astype(F32)).astype(BF16)
    dot = lambda p: jnp.dot(incl_bf16, p, preferred_element_type=F32)
    return dot(hi) + dot(mid) + dot(lo)


def _gdn_kernel(q_ref, k_ref, v_ref, gcol_ref, bcol_ref, grow_ref, s0_ref,
                o_ref, sout_ref, s_scr, *, chunk, heads, nseq):
    c = pl.program_id(1)

    @pl.when(c == 0)
    def _():
        s_scr[...] = s0_ref[...]

    ri = lax.broadcasted_iota(jnp.int32, (chunk, chunk), 0)
    ci = lax.broadcasted_iota(jnp.int32, (chunk, chunk), 1)
    causal = ri >= ci
    strict = ri > ci
    units = [(s, h) for s in range(nseq) for h in range(heads)]
    us = range(len(units))
    sl = [slice(h * GDN_DK, (h + 1) * GDN_DK) for _, h in units]
    q = [q_ref[s, :, sl[u]] for u, (s, h) in enumerate(units)]
    k = [k_ref[s, :, sl[u]] for u, (s, h) in enumerate(units)]
    v = [v_ref[s, :, sl[u]] for u, (s, h) in enumerate(units)]
    g_col = [gcol_ref[s, :, h:h + 1] for s, h in units]
    beta = [bcol_ref[s, :, h:h + 1] for s, h in units]
    g_row = [grow_ref[s, 0, h:h + 1, :] for s, h in units]
    G_col = [jnp.sum(jnp.where(causal, g_row[u], 0.0), axis=1, keepdims=True) for u in us]
    G_row = [jnp.sum(jnp.where(ri <= ci, g_col[u], 0.0), axis=0, keepdims=True) for u in us]
    g_last = [jnp.sum(g_row[u], axis=1, keepdims=True) for u in us]
    decay = [jnp.where(causal, jnp.exp(jnp.minimum(G_col[u] - G_row[u], 0.0)), 0.0) for u in us]
    eG = [jnp.exp(G_col[u]) for u in us]
    k_bf = [k[u].astype(BF16) for u in us]
    kb = [k[u] * beta[u] for u in us]
    S = [s_scr[s, h] for s, h in units]
    S_bf = [S[u].astype(BF16) for u in us]
    n_strict = [jnp.where(strict, _dot_nt(kb[u], k_bf[u]) * decay[u], 0.0) for u in us]
    qk = [(_dot_nt(q[u], k_bf[u]) * decay[u]).astype(BF16) for u in us]
    qS = [_dot(q[u] * eG[u], S_bf[u]) for u in us]
    x = [jnp.concatenate([v[u] * beta[u], kb[u] * eG[u]], axis=1) for u in us]
    p = n_strict
    x = [x[u] - _dot3(p[u], x[u]) for u in us]
    span = 2
    while span < chunk:
        dot = _dot3 if span < SOLVE_SPLIT_SPAN else _dot
        p = [dot(p[u], p[u]) for u in us]
        x = [x[u] + dot(p[u], x[u]) for u in us]
        span *= 2
    v_new = [x[u][:, :GDN_DV] - _dot(x[u][:, GDN_DV:], S_bf[u]) for u in us]
    v_new_bf = [v_new[u].astype(BF16) for u in us]
    for u, (s, h) in enumerate(units):
        o_ref[s, :, sl[u]] = qS[u] + _dot(qk[u], v_new_bf[u])
    for u, (s, h) in enumerate(units):
        k_dec = k[u] * jnp.exp(g_last[u] - G_col[u])
        s_scr[s, h] = S[u] * jnp.exp(g_last[u]) + _dot_tn(k_dec, v_new_bf[u])

    @pl.when(c == pl.num_programs(1) - 1)
    def _():
        sout_ref[...] = s_scr[...]


def gdn_recurrence(qkv, g, beta, S0, chunk, nseq):
    B, L, _ = qkv.shape
    H = GDN_HEADS
    n = L // chunk
    g_row = jnp.transpose(g, (0, 2, 1)).reshape(B, H, n, chunk).transpose(0, 2, 1, 3)
    blk = lambda part: pl.BlockSpec((nseq, chunk, H * GDN_DK), lambda b, c, part=part: (b, c, part))
    col = pl.BlockSpec((nseq, chunk, H), lambda b, c: (b, c, 0))
    st = pl.BlockSpec((nseq, H, GDN_DK, GDN_DV), lambda b, c: (b, 0, 0, 0))
    o, S = pl.pallas_call(
        functools.partial(_gdn_kernel, chunk=chunk, heads=H, nseq=nseq),
        grid=(B // nseq, n),
        in_specs=[blk(0), blk(1), blk(2), col, col,
                  pl.BlockSpec((nseq, 1, H, chunk), lambda b, c: (b, c, 0, 0)), st],
        out_specs=[blk(0), st],
        out_shape=[jax.ShapeDtypeStruct((B, L, H * GDN_DV), F32),
                   jax.ShapeDtypeStruct(S0.shape, F32)],
        scratch_shapes=[pltpu.VMEM((nseq, H, GDN_DK, GDN_DV), F32)],
        compiler_params=pltpu.CompilerParams(
            dimension_semantics=("parallel", "arbitrary"),
            vmem_limit_bytes=VMEM_LIMIT_BYTES),
        name="gdn_recurrence",
    )(qkv, qkv, qkv, g, beta, g_row, S0)
    return o, S


def _rwkv_kernel(r_ref, lw_ref, k_ref, v_ref, kk_ref, b_ref, s0_ref,
                 o_ref, sout_ref, s_scr, *, chunk, heads, nseq):
    c = pl.program_id(1)

    @pl.when(c == 0)
    def _():
        s_scr[...] = s0_ref[...]

    ri = lax.broadcasted_iota(jnp.int32, (chunk, chunk), 0)
    ci = lax.broadcasted_iota(jnp.int32, (chunk, chunk), 1)
    incl = (ri >= ci).astype(BF16)
    ri2 = lax.broadcasted_iota(jnp.int32, (2 * chunk, 2 * chunk), 0)
    ci2 = lax.broadcasted_iota(jnp.int32, (2 * chunk, 2 * chunk), 1)
    cj = jnp.where(ci2 >= chunk, ci2 - chunk, ci2)
    keep = ((ri2 < chunk) & (ri2 > cj)) | ((ri2 >= chunk) & ((ri2 - chunk) >= cj))

    rows_k, rows_r, cols_k, cols_b, k_dec, b_dec, v_bf, P_last = [], [], [], [], [], [], [], []
    for s in range(nseq):
        lw = lw_ref[s]
        LP = _cumsum_rows(incl, lw)
        LP_last = LP[chunk - 1:chunk, :]
        P_inv = jnp.exp(-LP)
        dec = jnp.exp(LP_last - LP)
        P_last.append(jnp.exp(LP_last))
        kk = kk_ref[s]
        b = b_ref[s]
        k = k_ref[s]
        rows_k.append((kk * jnp.exp(LP - lw)).astype(BF16))
        rows_r.append((r_ref[s] * jnp.exp(LP)).astype(BF16))
        cols_k.append((k * P_inv).astype(BF16))
        cols_b.append((b * P_inv).astype(BF16))
        k_dec.append((k * dec).astype(BF16))
        b_dec.append((b * dec).astype(BF16))
        v_bf.append(v_ref[s].astype(BF16))

    units = [(s, h) for s in range(nseq) for h in range(heads)]
    us = range(len(units))
    sl = [slice(h * RWKV_N, (h + 1) * RWKV_N) for _, h in units]
    rows = [jnp.concatenate([rows_k[s][:, sl[u]], rows_r[s][:, sl[u]]], axis=0) for u, (s, h) in enumerate(units)]
    cols = [jnp.concatenate([cols_k[s][:, sl[u]], cols_b[s][:, sl[u]]], axis=0) for u, (s, h) in enumerate(units)]
    S = [s_scr[s, h] for s, h in units]
    A = [jnp.where(keep, _dot_nt(rows[u], cols[u]), 0.0).astype(BF16) for u in us]
    base = [_dot_nt(rows[u], S[u]) for u in us]
    vh = [v_bf[s][:, sl[u]] for u, (s, h) in enumerate(units)]
    x = [base[u][:chunk] + _dot(A[u][:chunk, :chunk], vh[u]) for u in us]
    p = [A[u][:chunk, chunk:] for u in us]
    x = [x[u] - _dot(p[u], x[u]) for u in us]
    span = 2
    while span < chunk:
        p = [_dot(p[u], p[u]).astype(BF16) for u in us]
        x = [x[u] + _dot(p[u], x[u]) for u in us]
        span *= 2
    sol = [x[u].astype(BF16) for u in us]
    for u, (s, h) in enumerate(units):
        o_ref[s, :, sl[u]] = (base[u][chunk:] + _dot(A[u][chunk:, :chunk], vh[u])
                              - _dot(A[u][chunk:, chunk:], sol[u]))
    for u, (s, h) in enumerate(units):
        s_scr[s, h] = (S[u] * P_last[s][:, sl[u]] + _dot_tn(vh[u], k_dec[s][:, sl[u]])
                       - _dot_tn(sol[u], b_dec[s][:, sl[u]]))

    @pl.when(c == pl.num_programs(1) - 1)
    def _():
        sout_ref[...] = s_scr[...]


def rwkv_recurrence(r, lw, k, v, kk, b, S0, chunk, nseq):
    B, L, D = r.shape
    n = L // chunk
    heads = D // RWKV_N
    blk = pl.BlockSpec((nseq, chunk, D), lambda bb, c: (bb, c, 0))
    st = pl.BlockSpec((nseq, heads, RWKV_N, RWKV_N), lambda bb, c: (bb, 0, 0, 0))
    o, S = pl.pallas_call(
        functools.partial(_rwkv_kernel, chunk=chunk, heads=heads, nseq=nseq),
        grid=(B // nseq, n),
        in_specs=[blk] * 6 + [st],
        out_specs=[blk, st],
        out_shape=[jax.ShapeDtypeStruct((B, L, D), F32),
                   jax.ShapeDtypeStruct(S0.shape, F32)],
        scratch_shapes=[pltpu.VMEM((nseq, heads, RWKV_N, RWKV_N), F32)],
        compiler_params=pltpu.CompilerParams(
            dimension_semantics=("parallel", "arbitrary"),
            vmem_limit_bytes=VMEM_LIMIT_BYTES),
        name="rwkv_recurrence",
    )(r, lw, k, v, kk, b, S0)
    return o, S


ROW_TILES = (528, 512, 256, 128, 64, 32, 16, 8)
LAYER_VMEM_LIMIT_BYTES = 58 * 1024 * 1024


def _rms(x, eps):
    return x * lax.rsqrt(jnp.mean(x * x, axis=-1, keepdims=True) + eps)


def _softplus(x):
    return jnp.maximum(x, 0.0) + jnp.log1p(jnp.exp(-jnp.abs(x)))


def _sigmoid(x):
    return 1.0 / (1.0 + jnp.exp(-x))


def _exact_dot(x, const_bf16):
    hi, lo = _split2(x)
    dot = lambda p: jnp.dot(p, const_bf16, preferred_element_type=F32)
    return dot(hi) + dot(lo)


def _head_indicator(d, n):
    shift = n.bit_length() - 1
    assert 1 << shift == n
    ch = lax.shift_right_logical(lax.broadcasted_iota(jnp.int32, (d, 128), 0), shift)
    hd = lax.broadcasted_iota(jnp.int32, (d, 128), 1)
    cht = lax.shift_right_logical(lax.broadcasted_iota(jnp.int32, (128, d), 1), shift)
    hdt = lax.broadcasted_iota(jnp.int32, (128, d), 0)
    return (ch == hd).astype(BF16), (cht == hdt).astype(BF16)


def _shifted_rows(buf, carry_rows, cur, shift, taps, tm):
    prev = carry_rows.shape[0]
    pad = buf.shape[0] - tm
    buf[pad - prev:pad, :] = carry_rows
    buf[pad:, :] = cur
    return [buf[pad - j * shift:pad - j * shift + tm, :] for j in range(1, taps + 1)]


def _row_grid_call(kernel_fn, name, G, R, tm, in_specs, out_specs, out_shape, scratch_shapes):
    return pl.pallas_call(
        kernel_fn, grid=(G, R // tm), in_specs=in_specs, out_specs=out_specs, out_shape=out_shape,
        scratch_shapes=scratch_shapes,
        compiler_params=pltpu.CompilerParams(dimension_semantics=("parallel", "arbitrary"),
                                             vmem_limit_bytes=LAYER_VMEM_LIMIT_BYTES),
        name=name)


def _tile_spec(tm, c):
    return pl.BlockSpec((1, tm, c), lambda g, t: (g, t, 0))


def _group_spec(rows, c):
    return pl.BlockSpec((1, rows, c), lambda g, t: (g, 0, 0))


def _layer_spec(layer, *blk):
    return pl.BlockSpec((1,) + blk, lambda g, t: (layer,) + (0,) * len(blk))


def _state_row(R, tm, valid_rows, prev):
    row = valid_rows - prev - (R // tm - 1) * tm
    assert 0 <= row and prev <= tm, "the new conv/shift state must come from the last row tile"
    return row


FFN_COL_CHUNK = 256


def _ffn_kernel(x_ref, nw_ref, wup_ref, cw_ref, wdn_ref, cbuf_ref, fnw_ref,
                o_ref, cout_ref, gbuf, carry, *, tm, shift, d_ff, state_row, final_norm):
    t = pl.program_id(1)
    x = x_ref[0]
    hn = (_rms(x, RMS_EPS) * nw_ref[0]).astype(BF16)
    taps = cw_ref.shape[1] - 1
    prev = taps * shift

    @pl.when(t == 0)
    def _():
        carry[...] = cbuf_ref[0]

    acc = jnp.zeros(x.shape, F32)
    for c in range(d_ff // FFN_COL_CHUNK):
        cols = slice(c * FFN_COL_CHUNK, (c + 1) * FFN_COL_CHUNK)
        ucols = slice(d_ff + c * FFN_COL_CHUNK, d_ff + (c + 1) * FFN_COL_CHUNK)
        g = jnp.dot(hn, wup_ref[0, :, cols], preferred_element_type=F32)
        u = jnp.dot(hn, wup_ref[0, :, ucols], preferred_element_type=F32)
        delayed = _shifted_rows(gbuf.at[c % 2], carry[:, cols], g, shift, taps, tm)
        w = cw_ref[0, :, cols]
        gc = g * w[taps:taps + 1]
        for j in range(1, taps + 1):
            gc = gc + delayed[j - 1] * w[taps - j:taps - j + 1]
        act = (gc * _sigmoid(gc) * u).astype(BF16)
        acc = acc + jnp.dot(act, wdn_ref[0, cols, :], preferred_element_type=F32)
        carry[:, cols] = g[tm - prev:, :]
        cout_ref[0, :, cols] = g[state_row:state_row + prev, :]
    y = x + acc
    if final_norm:
        y = _rms(y, RMS_EPS) * fnw_ref[...]
    o_ref[0] = y


def ffn_layer(x, conv_state, p, layer, lay, final_norm):
    G, R, D = x.shape
    d_ff = p['ffn_conv_w'].shape[-1]
    taps = p['ffn_conv_w'].shape[1] - 1
    shift = lay['shift']
    prev = taps * shift
    tm = _pick_tile(R, ROW_TILES)
    pad = -(-prev // 8) * 8
    kern = functools.partial(_ffn_kernel, tm=tm, shift=shift, d_ff=d_ff,
                             state_row=_state_row(R, tm, lay['valid'], prev), final_norm=final_norm)
    return _row_grid_call(
        kern, "ffn_layer", G, R, tm,
        in_specs=[_tile_spec(tm, D), _layer_spec(layer, 1, D), _layer_spec(layer, D, 2 * d_ff),
                  _layer_spec(layer, taps + 1, d_ff), _layer_spec(layer, d_ff, D), _group_spec(prev, d_ff),
                  pl.BlockSpec((1, D), lambda g, t: (0, 0))],
        out_specs=[_tile_spec(tm, D), _group_spec(prev, d_ff)],
        out_shape=[jax.ShapeDtypeStruct((G, R, D), F32), jax.ShapeDtypeStruct((G, prev, d_ff), F32)],
        scratch_shapes=[pltpu.VMEM((2, pad + tm, FFN_COL_CHUNK), F32), pltpu.VMEM((prev, d_ff), F32)],
    )(x, p['norm_ffn3'], p['ffn_w_up'], p['ffn_conv_w'], p['ffn_w_down'], conv_state,
      p['norm_final'].reshape(1, D))


GDN_COL_CHUNK = 256


def _gdn_pre_kernel(x_ref, nw_ref, w_ref, wab_ref, cw_ref, al_ref, dt_ref, cbuf_ref,
                    qkv_ref, z_ref, g_ref, beta_ref, cout_ref, gbuf, carry,
                    *, tm, shift, conv_dim, state_row, valid_rows, heads):
    t = pl.program_id(1)
    hn = (_rms(x_ref[0], RMS_EPS) * nw_ref[0]).astype(BF16)
    taps = cw_ref.shape[1] - 1
    prev = taps * shift

    @pl.when(t == 0)
    def _():
        carry[...] = cbuf_ref[0]

    qk_dim = heads * GDN_DK
    for c in range(conv_dim // GDN_COL_CHUNK):
        cols = slice(c * GDN_COL_CHUNK, (c + 1) * GDN_COL_CHUNK)
        pre = jnp.dot(hn, w_ref[0, :, cols], preferred_element_type=F32)
        delayed = _shifted_rows(gbuf.at[c % 2], carry[:, cols], pre, shift, taps, tm)
        w = cw_ref[0, :, cols]
        acc = pre * w[taps:taps + 1]
        for j in range(1, taps + 1):
            acc = acc + delayed[j - 1] * w[taps - j:taps - j + 1]
        act = acc * _sigmoid(acc)
        carry[:, cols] = pre[tm - prev:, :]
        cout_ref[0, :, cols] = pre[state_row:state_row + prev, :]
        if c * GDN_COL_CHUNK < 2 * qk_dim:
            scale = GDN_DK ** -0.5 if c * GDN_COL_CHUNK < qk_dim else 1.0
            for h in range(GDN_COL_CHUNK // GDN_DK):
                a_h = act[:, h * GDN_DK:(h + 1) * GDN_DK]
                n_h = a_h * lax.rsqrt(jnp.sum(a_h * a_h, axis=-1, keepdims=True) + L2_EPS)
                lo = c * GDN_COL_CHUNK + h * GDN_DK
                qkv_ref[0, :, lo:lo + GDN_DK] = n_h * scale if scale != 1.0 else n_h
        else:
            qkv_ref[0, :, cols] = act
    z_dim = z_ref.shape[-1]
    for c in range(z_dim // GDN_COL_CHUNK):
        cols = slice(c * GDN_COL_CHUNK, (c + 1) * GDN_COL_CHUNK)
        wcols = slice(conv_dim + c * GDN_COL_CHUNK, conv_dim + (c + 1) * GDN_COL_CHUNK)
        z = jnp.dot(hn, w_ref[0, :, wcols], preferred_element_type=F32)
        z_ref[0, :, cols] = z * _sigmoid(z)
    ab = jnp.dot(hn, wab_ref[0], preferred_element_type=F32)
    row = t * tm + lax.broadcasted_iota(jnp.int32, (tm, 1), 0)
    m = (row < valid_rows).astype(F32)
    g_ref[0] = -jnp.exp(al_ref[0]) * _softplus(ab[:, :heads] + dt_ref[0]) * m
    beta_ref[0] = _sigmoid(ab[:, heads:2 * heads]) * m


def gdn_pre(x, conv_state, p, i, layer, lay):
    G, R, D = x.shape
    H = GDN_HEADS
    conv_dim = p['gdn_conv_w'].shape[-1]
    z_dim = p['gdn_w_main'].shape[-1] - conv_dim
    taps = p['gdn_conv_w'].shape[1] - 1
    shift = lay['shift']
    prev = taps * shift
    tm = _pick_tile(R, ROW_TILES)
    pad = -(-prev // 8) * 8
    kern = functools.partial(_gdn_pre_kernel, tm=tm, shift=shift, conv_dim=conv_dim,
                             state_row=_state_row(R, tm, lay['valid'], prev), valid_rows=lay['valid'], heads=H)
    return _row_grid_call(
        kern, "gdn_pre", G, R, tm,
        in_specs=[_tile_spec(tm, D), _layer_spec(i, 1, D), _layer_spec(layer, D, conv_dim + z_dim),
                  _layer_spec(layer, D, 128), _layer_spec(layer, taps + 1, conv_dim),
                  _layer_spec(layer, 1, H), _layer_spec(layer, 1, H), _group_spec(prev, conv_dim)],
        out_specs=[_tile_spec(tm, conv_dim), _tile_spec(tm, z_dim), _tile_spec(tm, H), _tile_spec(tm, H),
                   _group_spec(prev, conv_dim)],
        out_shape=[jax.ShapeDtypeStruct((G, R, conv_dim), F32), jax.ShapeDtypeStruct((G, R, z_dim), F32),
                   jax.ShapeDtypeStruct((G, R, H), F32), jax.ShapeDtypeStruct((G, R, H), F32),
                   jax.ShapeDtypeStruct((G, prev, conv_dim), F32)],
        scratch_shapes=[pltpu.VMEM((2, pad + tm, GDN_COL_CHUNK), F32), pltpu.VMEM((prev, conv_dim), F32)],
    )(x, p['norm_mix3'], p['gdn_w_main'], p['gdn_w_ab'], p['gdn_conv_w'], p['gdn_A_log3'], p['gdn_dt_bias3'], conv_state)


def _gdn_post_kernel(x_ref, o_ref, z_ref, nw_ref, w_ref, y_ref, *, heads):
    o = o_ref[0]
    parts = []
    for h in range(heads):
        o_h = o[:, h * GDN_DV:(h + 1) * GDN_DV]
        parts.append(_rms(o_h, RMS_EPS) * nw_ref[0])
    gated = (jnp.concatenate(parts, axis=1) * z_ref[0]).astype(BF16)
    y_ref[0] = x_ref[0] + jnp.dot(gated, w_ref[0], preferred_element_type=F32)


def gdn_post(x, o, z_act, p, layer):
    G, R, D = x.shape
    v_dim = o.shape[-1]
    tm = _pick_tile(R, ROW_TILES)
    return _row_grid_call(
        functools.partial(_gdn_post_kernel, heads=GDN_HEADS), "gdn_post", G, R, tm,
        in_specs=[_tile_spec(tm, D), _tile_spec(tm, v_dim), _tile_spec(tm, v_dim),
                  _layer_spec(layer, 1, GDN_DV), _layer_spec(layer, v_dim, D)],
        out_specs=_tile_spec(tm, D),
        out_shape=jax.ShapeDtypeStruct((G, R, D), F32),
        scratch_shapes=[],
    )(x, o, z_act, p['gdn_norm_w3'], p['gdn_w_out'])


RWKV_ROW_TILES = (352, 256, 128, 64, 32, 16, 8)


def _rwkv_pre_kernel(*refs, tm, shift, state_row, valid_rows, has_vres):
    (x_ref, nw_ref, mix_ref, wr_ref, wk_ref, wv_ref, w1_ref, w2_ref, a1_ref, a2_ref, g1_ref, g2_ref,
     w0_ref, a0_ref, kk_ref, ka_ref, rk_ref, sh_ref) = refs[:18]
    refs = refs[18:]
    if has_vres:
        v1_ref, v2_ref, v0_ref, vf_ref = refs[:4]
        refs = refs[4:]
    (r_ref, lw_ref, kf_ref, v_ref, kkn_ref, b_ref, gate_ref, bonus_ref, shout_ref, hbuf, carry) = refs
    t = pl.program_id(1)

    @pl.when(t == 0)
    def _():
        carry[...] = sh_ref[0]

    hn = _rms(x_ref[0], RMS_EPS) * nw_ref[0]
    prev = _shifted_rows(hbuf, carry[...], hn, shift, 1, tm)[0]
    carry[...] = hn[tm - shift:, :]
    shout_ref[0] = hn[state_row:state_row + shift, :]
    xx = prev - hn
    mixed = lambda i: (hn + xx * mix_ref[0, i:i + 1, :]).astype(BF16)
    dot = lambda a, w_ref: jnp.dot(a, w_ref[0], preferred_element_type=F32)
    xr, xw, xk, xv, xa, xg = (mixed(i) for i in range(6))
    r = dot(xr, wr_ref)
    k = dot(xk, wk_ref)
    v = dot(xv, wv_ref)
    w_lora = dot(jnp.tanh(dot(xw, w1_ref)).astype(BF16), w2_ref)
    w_log = -_softplus(-(w0_ref[0] + w_lora)) - 0.5
    if has_vres:
        v_lora = dot(dot(xv, v1_ref).astype(BF16), v2_ref)
        v = v + (vf_ref[0] - v) * _sigmoid(v0_ref[0] + v_lora)
    a = _sigmoid(a0_ref[0] + dot(dot(xa, a1_ref).astype(BF16), a2_ref))
    gate_ref[0] = dot(_sigmoid(dot(xg, g1_ref)).astype(BF16), g2_ref)

    D = hn.shape[-1]
    seg, seg_t = _head_indicator(D, RWKV_N)
    kkk = k * kk_ref[0]
    rs = lax.rsqrt(_exact_dot(kkk * kkk, seg) + L2_EPS)
    kkn = kkk * _exact_dot(rs, seg_t)
    kf = k * (1.0 + (a - 1.0) * ka_ref[0])
    bonus_ref[0] = _exact_dot(r * kf * rk_ref[0], seg)

    row = t * tm + lax.broadcasted_iota(jnp.int32, (tm, 1), 0)
    m = (row < valid_rows).astype(F32)
    kkn = kkn * m
    r_ref[0] = r
    lw_ref[0] = -jnp.exp(w_log) * m
    kf_ref[0] = kf * m
    v_ref[0] = v * m
    kkn_ref[0] = kkn
    b_ref[0] = kkn * a


def rwkv_pre(x, shift_state, v_first, p, i, layer, lay):
    G, R, D = x.shape
    s = lay['shift']
    tm = _pick_tile(R, RWKV_ROW_TILES)
    pad = -(-s // 8) * 8
    has_vres = layer > 0
    L = lambda *blk: _layer_spec(layer, *blk)
    lora = lambda name: p[name].shape[-1]
    in_specs = [_tile_spec(tm, D), _layer_spec(i, 1, D), L(6, D), L(D, D), L(D, D), L(D, D),
                L(D, lora('rwkv_w1')), L(lora('rwkv_w1'), D), L(D, lora('rwkv_a1')), L(lora('rwkv_a1'), D),
                L(D, lora('rwkv_g1')), L(lora('rwkv_g1'), D), L(1, D), L(1, D), L(1, D), L(1, D), L(1, D),
                _group_spec(s, D)]
    args = [x, p['norm_mix3'], p['rwkv_mix'], p['rwkv_wr'], p['rwkv_wk'], p['rwkv_wv'], p['rwkv_w1'],
            p['rwkv_w2'], p['rwkv_a1'], p['rwkv_a2'], p['rwkv_g1'], p['rwkv_g2'], p['rwkv_w03'], p['rwkv_a03'],
            p['rwkv_k_k3'], p['rwkv_k_a3'], p['rwkv_r_k3'], shift_state]
    if has_vres:
        V = lambda *blk: _layer_spec(layer - 1, *blk)
        in_specs += [V(D, lora('rwkv_v1')), V(lora('rwkv_v1'), D), V(1, D), _tile_spec(tm, D)]
        args += [p['rwkv_v1'], p['rwkv_v2'], p['rwkv_v03'], v_first]
    act = jax.ShapeDtypeStruct((G, R, D), F32)
    kern = functools.partial(_rwkv_pre_kernel, tm=tm, shift=s, state_row=_state_row(R, tm, lay['valid'], s),
                             valid_rows=lay['valid'], has_vres=has_vres)
    return _row_grid_call(
        kern, "rwkv_pre", G, R, tm, in_specs=in_specs,
        out_specs=[_tile_spec(tm, D)] * 7 + [_tile_spec(tm, 128), _group_spec(s, D)],
        out_shape=[act] * 7 + [jax.ShapeDtypeStruct((G, R, 128), F32), jax.ShapeDtypeStruct((G, s, D), F32)],
        scratch_shapes=[pltpu.VMEM((pad + tm, D), F32), pltpu.VMEM((s, D), F32)],
    )(*args)


def _rwkv_post_kernel(x_ref, y_ref, v_ref, gate_ref, bonus_ref, lnw_ref, lnb_ref, wo_ref, o_ref):
    y = y_ref[0]
    D = y.shape[-1]
    seg, seg_t = _head_indicator(D, RWKV_N)
    inv_n = 1.0 / RWKV_N
    mu = _exact_dot(_exact_dot(y, seg) * inv_n, seg_t)
    d = y - mu
    rs = lax.rsqrt(_exact_dot(d * d, seg) * inv_n + GN_EPS)
    yn = d * _exact_dot(rs, seg_t) * lnw_ref[0] + lnb_ref[0]
    yn = yn + _exact_dot(bonus_ref[0], seg_t) * v_ref[0]
    o_ref[0] = x_ref[0] + jnp.dot((yn * gate_ref[0]).astype(BF16), wo_ref[0], preferred_element_type=F32)


def rwkv_post(x, y, v, gate, bonus, p, layer):
    G, R, D = x.shape
    tm = _pick_tile(R, RWKV_ROW_TILES)
    L = lambda *blk: _layer_spec(layer, *blk)
    return _row_grid_call(
        _rwkv_post_kernel, "rwkv_post", G, R, tm,
        in_specs=[_tile_spec(tm, D)] * 4 + [_tile_spec(tm, 128), L(1, D), L(1, D), L(D, D)],
        out_specs=_tile_spec(tm, D),
        out_shape=jax.ShapeDtypeStruct((G, R, D), F32),
        scratch_shapes=[],
    )(x, y, v, gate, bonus, p['rwkv_ln_w3'], p['rwkv_ln_b3'], p['rwkv_wo'])


def _to_seq(a, lay):
    if lay['shift'] == 1:
        return a
    B = lay['shift']
    T = a.shape[1] // B
    a = a.reshape(T, B, a.shape[-1]).transpose(1, 0, 2)
    return jnp.pad(a, ((0, 0), (0, lay['chunk'] - T), (0, 0)))


def _from_seq(o, lay):
    if lay['shift'] == 1:
        return o
    B = lay['shift']
    T = lay['valid'] // B
    return o[:, :T].transpose(1, 0, 2).reshape(1, T * B, o.shape[-1])


def _gdn_mixer(x, conv_buf, S0, p, i, j, lay):
    qkv_n, z_act, g, beta, new_buf = gdn_pre(x, conv_buf, p, i, j, lay)
    o, S = gdn_recurrence(_to_seq(qkv_n, lay), _to_seq(g, lay), _to_seq(beta, lay), S0, lay['chunk'], lay['nseq'])
    return gdn_post(x, _from_seq(o, lay), z_act, p, j), new_buf, S


def _rwkv_mixer(x, shift, S0, v_first, p, i, j, lay):
    r, lw, kf, v, kk, b, gate, bonus, new_shift = rwkv_pre(x, shift, v_first, p, i, j, lay)
    seq = lambda t: _to_seq(t, lay)
    y, S = rwkv_recurrence(seq(r), seq(lw), seq(kf), seq(v), seq(kk), seq(b), S0, lay['chunk'], lay['nseq'])
    if j == 0:
        v_first = v
    return rwkv_post(x, _from_seq(y, lay), v, gate, bonus, p, j), new_shift, S, v_first


def _trunk(x, gdn_S, gdn_conv, rwkv_S, rwkv_shift, ffn_conv, p, lay):
    depth = p['depth']
    gS, gC, rS, rSh, fC = [], [], [], [], []
    v_first = None
    for i in range(depth):
        j = i // 2
        if i % 2 == 0:
            x, cb, S = _gdn_mixer(x, gdn_conv[j], gdn_S[j], p, i, j, lay)
            gS.append(S)
            gC.append(cb)
        else:
            x, sh, S, v_first = _rwkv_mixer(x, rwkv_shift[j], rwkv_S[j], v_first, p, i, j, lay)
            rS.append(S)
            rSh.append(sh)
        x, fb = ffn_layer(x, ffn_conv[i], p, i, lay, final_norm=(i == depth - 1))
        fC.append(fb)
    return x, jnp.stack(gS), jnp.stack(gC), jnp.stack(rS), jnp.stack(rSh), jnp.stack(fC)


def _pad_tokens(x, chunk):
    L = x.shape[1]
    Lp = -(-L // chunk) * chunk
    return jnp.pad(x, ((0, 0), (0, Lp - L), (0, 0)))


def _rows_state(s):
    N, B, W, C = s.shape
    return s.transpose(0, 2, 1, 3).reshape(N, 1, W * B, C)


def _seq_state(s, B):
    N, _, WB, C = s.shape
    return s.reshape(N, WB // B, B, C).transpose(0, 2, 1, 3)


def kernel(x_prompt, x_sample, state_gdn_S, state_gdn_conv, state_rwkv_S, state_rwkv_shift, state_ffn_conv, meta_tokens, norm_mix, norm_ffn, norm_final, gdn_w_in, gdn_conv_w, gdn_A_log, gdn_dt_bias, gdn_norm_w, gdn_w_out, rwkv_mix, rwkv_wr, rwkv_wk, rwkv_wv, rwkv_wo, rwkv_w0, rwkv_w1, rwkv_w2, rwkv_a0, rwkv_a1, rwkv_a2, rwkv_g1, rwkv_g2, rwkv_k_k, rwkv_k_a, rwkv_r_k, rwkv_ln_w, rwkv_ln_b, rwkv_v0, rwkv_v1, rwkv_v2, ffn_w_up, ffn_conv_w, ffn_w_down):
    n_main = gdn_w_in.shape[-1] - 2 * GDN_HEADS
    w_ab = jnp.pad(gdn_w_in[..., n_main:], ((0, 0), (0, 0), (0, 128 - 2 * GDN_HEADS)))
    bf = lambda w: w.astype(BF16)
    three = lambda a: a.reshape(a.shape[0], 1, a.shape[-1])
    p = dict(depth=norm_mix.shape[0], norm_mix3=three(norm_mix), norm_ffn3=three(norm_ffn), norm_final=norm_final,
             gdn_w_main=bf(gdn_w_in[..., :n_main]), gdn_w_ab=bf(w_ab), gdn_conv_w=gdn_conv_w,
             gdn_A_log3=three(gdn_A_log), gdn_dt_bias3=three(gdn_dt_bias), gdn_norm_w3=three(gdn_norm_w),
             gdn_w_out=bf(gdn_w_out), rwkv_mix=rwkv_mix, rwkv_wr=bf(rwkv_wr), rwkv_wk=bf(rwkv_wk),
             rwkv_wv=bf(rwkv_wv), rwkv_wo=bf(rwkv_wo), rwkv_w1=bf(rwkv_w1), rwkv_w2=bf(rwkv_w2),
             rwkv_a1=bf(rwkv_a1), rwkv_a2=bf(rwkv_a2), rwkv_g1=bf(rwkv_g1), rwkv_g2=bf(rwkv_g2),
             rwkv_v1=bf(rwkv_v1), rwkv_v2=bf(rwkv_v2), rwkv_w03=three(rwkv_w0), rwkv_a03=three(rwkv_a0),
             rwkv_v03=three(rwkv_v0), rwkv_k_k3=three(rwkv_k_k), rwkv_k_a3=three(rwkv_k_a),
             rwkv_r_k3=rwkv_r_k.reshape(rwkv_r_k.shape[0], 1, -1), rwkv_ln_w3=three(rwkv_ln_w),
             rwkv_ln_b3=three(rwkv_ln_b), ffn_w_up=bf(ffn_w_up), ffn_conv_w=ffn_conv_w, ffn_w_down=bf(ffn_w_down))

    B = x_prompt.shape[0]
    D = x_prompt.shape[-1]
    meta = jnp.broadcast_to(meta_tokens[None], (B, N_META, D))
    xp = jnp.concatenate([meta, x_prompt], axis=1)
    Lp = xp.shape[1]
    zeros_like_batch = lambda s: jnp.zeros((s.shape[0], B) + s.shape[2:], F32)
    yp, gS_p, gC_p, rS_p, rSh_p, fC_p = _trunk(
        _pad_tokens(xp, PROMPT_CHUNK), zeros_like_batch(state_gdn_S), zeros_like_batch(state_gdn_conv),
        zeros_like_batch(state_rwkv_S), zeros_like_batch(state_rwkv_shift)[:, :, None],
        zeros_like_batch(state_ffn_conv), p, dict(shift=1, valid=Lp, chunk=PROMPT_CHUNK, nseq=1))
    y_prompt = yp[:, N_META:Lp]
    rSh_p = rSh_p[:, :, 0]

    Bs, Ls, _ = x_sample.shape
    xs = x_sample.transpose(1, 0, 2).reshape(1, Ls * Bs, D)
    ys, gS_s, gC_s, rS_s, rSh_s, fC_s = _trunk(
        xs, state_gdn_S, _rows_state(state_gdn_conv), state_rwkv_S, state_rwkv_shift[:, None],
        _rows_state(state_ffn_conv), p, dict(shift=Bs, valid=Ls * Bs, chunk=SAMPLE_CHUNK, nseq=_pick_tile(Bs, (SAMPLE_SEQS_PER_STEP, 2, 1))))
    y_sample = ys.reshape(Ls, Bs, D).transpose(1, 0, 2)
    gC_s = _seq_state(gC_s, Bs)
    fC_s = _seq_state(fC_s, Bs)
    rSh_s = rSh_s[:, 0]
    return (y_prompt, y_sample, gS_p, gC_p, rS_p, rSh_p, fC_p, gS_s, gC_s, rS_s, rSh_s, fC_s)
```

```python
import functools

import jax
import jax.numpy as jnp
from jax import lax
from jax.experimental import pallas as pl
from jax.experimental.pallas import tpu as pltpu

F32 = jnp.float32
BF16 = jnp.bfloat16

N_META = 16
GDN_HEADS = 8
GDN_DK = 128
GDN_DV = 128
RWKV_N = 64
RMS_EPS = 1e-6
L2_EPS = 1e-6
GN_EPS = 64e-5

PROMPT_CHUNK = 64
SAMPLE_CHUNK = 8
SAMPLE_SEQS_PER_STEP = 8
PROMPT_GDN_SEQS_PER_STEP = 4
PROMPT_RWKV_SEQS_PER_STEP = 2
SOLVE_SPLIT_SPAN = 8
VMEM_LIMIT_BYTES = 48 * 1024 * 1024


def _pick_tile(n, cands):
    for c in cands:
        if n % c == 0:
            return c
    return n


def _dot(a, b):
    return jnp.dot(a.astype(BF16), b.astype(BF16), preferred_element_type=F32)


def _dot_nt(a, b):
    return lax.dot_general(a.astype(BF16), b.astype(BF16), (((1,), (1,)), ((), ())),
                           preferred_element_type=F32)


def _dot_tn(a, b):
    return lax.dot_general(a.astype(BF16), b.astype(BF16), (((0,), (0,)), ((), ())),
                           preferred_element_type=F32)


def _split2(x):
    hi = x.astype(BF16)
    return hi, (x - hi.astype(F32)).astype(BF16)


def _dot3(a, b):
    ah, al = _split2(a)
    bh, bl = _split2(b)
    d = lambda x, y: jnp.dot(x, y, preferred_element_type=F32)
    return d(ah, bh) + (d(ah, bl) + d(al, bh))


LANES = 128
HALF_LANES = 64


def _stacked_state_spec(layers, nseq, state_shape):
    return pl.BlockSpec((layers, nseq) + tuple(state_shape), lambda b, c: (0, b) + (0,) * len(state_shape))


def _write_stacked_states(sout_ref, earlier_refs, s_scr):
    for i, ref in enumerate(earlier_refs):
        sout_ref[i] = ref[...]
    sout_ref[len(earlier_refs)] = s_scr[...]


def _pad_lanes(a):
    return jnp.concatenate([a, jnp.zeros(a.shape[:-1] + (LANES - a.shape[-1],), a.dtype)], axis=-1)


def _pad_lanes_left(a):
    return jnp.concatenate([jnp.zeros(a.shape[:-1] + (LANES - a.shape[-1],), a.dtype), a], axis=-1)


def _cumsum_rows(incl_bf16, x):
    hi = x.astype(BF16)
    r1 = x - hi.astype(F32)
    mid = r1.astype(BF16)
    lo = (r1 - mid.astype(F32)).astype(BF16)
    dot = lambda p: jnp.dot(incl_bf16, p, preferred_element_type=F32)
    return dot(hi) + dot(mid) + dot(lo)


def _gdn_kernel(q_ref, k_ref, v_ref, gcol_ref, bcol_ref, grow_ref, s0_ref, *rest, chunk, heads, nseq):
    earlier_refs, (o_ref, sout_ref, s_scr) = rest[:-3], rest[-3:]
    c = pl.program_id(1)

    @pl.when(c == 0)
    def _():
        s_scr[...] = s0_ref[...]

    ri = lax.broadcasted_iota(jnp.int32, (chunk, chunk), 0)
    ci = lax.broadcasted_iota(jnp.int32, (chunk, chunk), 1)
    causal = ri >= ci
    strict = ri > ci
    units = [(s, h) for s in range(nseq) for h in range(heads)]
    us = range(len(units))
    sl = [slice(h * GDN_DK, (h + 1) * GDN_DK) for _, h in units]
    q = [q_ref[s, :, sl[u]] for u, (s, h) in enumerate(units)]
    k = [k_ref[s, :, sl[u]] for u, (s, h) in enumerate(units)]
    v = [v_ref[s, :, sl[u]] for u, (s, h) in enumerate(units)]
    g_col = [gcol_ref[s, :, h:h + 1] for s, h in units]
    beta = [bcol_ref[s, :, h:h + 1] for s, h in units]
    g_row = [grow_ref[s, 0, h:h + 1, :] for s, h in units]
    G_col = [jnp.sum(jnp.where(causal, g_row[u], 0.0), axis=1, keepdims=True) for u in us]
    G_row = [jnp.sum(jnp.where(ri <= ci, g_col[u], 0.0), axis=0, keepdims=True) for u in us]
    g_last = [jnp.sum(g_row[u], axis=1, keepdims=True) for u in us]
    decay = [jnp.where(causal, jnp.exp(jnp.minimum(G_col[u] - G_row[u], 0.0)), 0.0) for u in us]
    eG = [jnp.exp(G_col[u]) for u in us]
    k_bf = [k[u].astype(BF16) for u in us]
    kb = [k[u] * beta[u] for u in us]
    S = [s_scr[s, h] for s, h in units]
    S_bf = [S[u].astype(BF16) for u in us]
    n_strict = [jnp.where(strict, _dot_nt(kb[u], k_bf[u]) * decay[u], 0.0) for u in us]
    qk = [(_dot_nt(q[u], k_bf[u]) * decay[u]).astype(BF16) for u in us]
    qS = [_dot(q[u] * eG[u], S_bf[u]) for u in us]
    lane = lax.broadcasted_iota(jnp.int32, (chunk, LANES), 1)
    row = lax.broadcasted_iota(jnp.int32, (chunk, LANES), 0)
    eye_hi = (lane == row + HALF_LANES).astype(F32)
    is_p = lane < HALF_LANES
    Z = [_pad_lanes(n_strict[u]) + eye_hi for u in us]
    span = 1
    sign = -1.0
    while span < chunk:
        dot = _dot3 if span < SOLVE_SPLIT_SPAN else _dot
        PZ = [dot(Z[u][:, :chunk], Z[u]) for u in us]
        Z = [jnp.where(is_p, PZ[u], Z[u] + sign * PZ[u]) for u in us]
        sign = 1.0
        span *= 2
    rhs = [jnp.concatenate([v[u] * beta[u], kb[u] * eG[u]], axis=1) for u in us]
    x = [_dot3(Z[u][:, HALF_LANES:HALF_LANES + chunk], rhs[u]) for u in us]
    v_new = [x[u][:, :GDN_DV] - _dot(x[u][:, GDN_DV:], S_bf[u]) for u in us]
    v_new_bf = [v_new[u].astype(BF16) for u in us]
    for u, (s, h) in enumerate(units):
        o_ref[s, :, sl[u]] = qS[u] + _dot(qk[u], v_new_bf[u])
    for u, (s, h) in enumerate(units):
        k_dec = k[u] * jnp.exp(g_last[u] - G_col[u])
        s_scr[s, h] = S[u] * jnp.exp(g_last[u]) + _dot_tn(k_dec, v_new_bf[u])

    @pl.when(c == pl.num_programs(1) - 1)
    def _():
        _write_stacked_states(sout_ref, earlier_refs, s_scr)


def gdn_recurrence(qkv, g, beta, S0, earlier, chunk, nseq):
    B, L, _ = qkv.shape
    H = GDN_HEADS
    n = L // chunk
    g_row = jnp.transpose(g, (0, 2, 1)).reshape(B, H, n, chunk).transpose(0, 2, 1, 3)
    blk = lambda part: pl.BlockSpec((nseq, chunk, H * GDN_DK), lambda b, c, part=part: (b, c, part))
    col = pl.BlockSpec((nseq, chunk, H), lambda b, c: (b, c, 0))
    st = pl.BlockSpec((nseq, H, GDN_DK, GDN_DV), lambda b, c: (b, 0, 0, 0))
    o, S = pl.pallas_call(
        functools.partial(_gdn_kernel, chunk=chunk, heads=H, nseq=nseq),
        grid=(B // nseq, n),
        in_specs=[blk(0), blk(1), blk(2), col, col,
                  pl.BlockSpec((nseq, 1, H, chunk), lambda b, c: (b, c, 0, 0)), st] + [st] * len(earlier),
        out_specs=[blk(0), _stacked_state_spec(len(earlier) + 1, nseq, S0.shape[1:])],
        out_shape=[jax.ShapeDtypeStruct((B, L, H * GDN_DV), F32),
                   jax.ShapeDtypeStruct((len(earlier) + 1,) + S0.shape, F32)],
        scratch_shapes=[pltpu.VMEM((nseq, H, GDN_DK, GDN_DV), F32)],
        compiler_params=pltpu.CompilerParams(
            dimension_semantics=("parallel", "arbitrary"),
            vmem_limit_bytes=VMEM_LIMIT_BYTES),
        name="gdn_recurrence",
    )(qkv, qkv, qkv, g, beta, g_row, S0, *earlier)
    return o, S


def _rwkv_kernel(r_ref, lw_ref, k_ref, v_ref, kk_ref, b_ref, s0_ref, *rest, chunk, heads, nseq):
    earlier_refs, (o_ref, sout_ref, s_scr) = rest[:-3], rest[-3:]
    c = pl.program_id(1)

    @pl.when(c == 0)
    def _():
        s_scr[...] = s0_ref[...]

    ri = lax.broadcasted_iota(jnp.int32, (chunk, chunk), 0)
    ci = lax.broadcasted_iota(jnp.int32, (chunk, chunk), 1)
    incl = (ri >= ci).astype(BF16)
    ri2 = lax.broadcasted_iota(jnp.int32, (2 * chunk, 2 * chunk), 0)
    ci2 = lax.broadcasted_iota(jnp.int32, (2 * chunk, 2 * chunk), 1)
    cj = jnp.where(ci2 >= chunk, ci2 - chunk, ci2)
    keep = ((ri2 < chunk) & (ri2 > cj)) | ((ri2 >= chunk) & ((ri2 - chunk) >= cj))

    rows_k, rows_r, cols_k, cols_b, k_dec, b_dec, v_bf, P_last = [], [], [], [], [], [], [], []
    for s in range(nseq):
        lw = lw_ref[s]
        LP = _cumsum_rows(incl, lw)
        LP_last = LP[chunk - 1:chunk, :]
        P_inv = jnp.exp(-LP)
        dec = jnp.exp(LP_last - LP)
        P_last.append(jnp.exp(LP_last))
        kk = kk_ref[s]
        b = b_ref[s]
        k = k_ref[s]
        rows_k.append((kk * jnp.exp(LP - lw)).astype(BF16))
        rows_r.append((r_ref[s] * jnp.exp(LP)).astype(BF16))
        cols_k.append((k * P_inv).astype(BF16))
        cols_b.append((b * P_inv).astype(BF16))
        k_dec.append((k * dec).astype(BF16))
        b_dec.append((b * dec).astype(BF16))
        v_bf.append(v_ref[s].astype(BF16))

    units = [(s, h) for s in range(nseq) for h in range(heads)]
    us = range(len(units))
    sl = [slice(h * RWKV_N, (h + 1) * RWKV_N) for _, h in units]
    rows = [jnp.concatenate([rows_k[s][:, sl[u]], rows_r[s][:, sl[u]]], axis=0) for u, (s, h) in enumerate(units)]
    cols = [jnp.concatenate([cols_k[s][:, sl[u]], cols_b[s][:, sl[u]]], axis=0) for u, (s, h) in enumerate(units)]
    S = [s_scr[s, h] for s, h in units]
    A = [jnp.where(keep, _dot_nt(rows[u], cols[u]), 0.0).astype(BF16) for u in us]
    base = [_dot_nt(rows[u], S[u]) for u in us]
    vh = [v_bf[s][:, sl[u]] for u, (s, h) in enumerate(units)]
    lane = lax.broadcasted_iota(jnp.int32, (chunk, LANES), 1)
    is_p = lane < HALF_LANES
    zero_v = jnp.zeros((chunk, HALF_LANES), BF16)
    Av = [_dot(A[u][:, :chunk], jnp.concatenate([zero_v, vh[u]], axis=1)) for u in us]
    Z = [jnp.where(is_p, _pad_lanes(A[u][:chunk, chunk:]).astype(F32),
                   _pad_lanes_left(base[u][:chunk]) + Av[u][:chunk]) for u in us]
    span = 1
    sign = -1.0
    while span < chunk:
        PZ = [_dot(Z[u][:, :chunk], Z[u]) for u in us]
        Z = [jnp.where(is_p, PZ[u], Z[u] + sign * PZ[u]) for u in us]
        sign = 1.0
        span *= 2
    sol = [Z[u][:, HALF_LANES:].astype(BF16) for u in us]
    for u, (s, h) in enumerate(units):
        o_ref[s, :, sl[u]] = (base[u][chunk:] + Av[u][chunk:, HALF_LANES:]
                              - _dot(A[u][chunk:, chunk:], sol[u]))
    for u, (s, h) in enumerate(units):
        s_scr[s, h] = (S[u] * P_last[s][:, sl[u]] + _dot_tn(vh[u], k_dec[s][:, sl[u]])
                       - _dot_tn(sol[u], b_dec[s][:, sl[u]]))

    @pl.when(c == pl.num_programs(1) - 1)
    def _():
        _write_stacked_states(sout_ref, earlier_refs, s_scr)


def rwkv_recurrence(r, lw, k, v, kk, b, S0, earlier, chunk, nseq):
    B, L, D = r.shape
    n = L // chunk
    heads = D // RWKV_N
    blk = pl.BlockSpec((nseq, chunk, D), lambda bb, c: (bb, c, 0))
    st = pl.BlockSpec((nseq, heads, RWKV_N, RWKV_N), lambda bb, c: (bb, 0, 0, 0))
    o, S = pl.pallas_call(
        functools.partial(_rwkv_kernel, chunk=chunk, heads=heads, nseq=nseq),
        grid=(B // nseq, n),
        in_specs=[blk] * 6 + [st] * (1 + len(earlier)),
        out_specs=[blk, _stacked_state_spec(len(earlier) + 1, nseq, S0.shape[1:])],
        out_shape=[jax.ShapeDtypeStruct((B, L, D), F32),
                   jax.ShapeDtypeStruct((len(earlier) + 1,) + S0.shape, F32)],
        scratch_shapes=[pltpu.VMEM((nseq, heads, RWKV_N, RWKV_N), F32)],
        compiler_params=pltpu.CompilerParams(
            dimension_semantics=("parallel", "arbitrary"),
            vmem_limit_bytes=VMEM_LIMIT_BYTES),
        name="rwkv_recurrence",
    )(r, lw, k, v, kk, b, S0, *earlier)
    return o, S


ROW_TILES = (528, 512, 256, 128, 64, 32, 16, 8)
LAYER_VMEM_LIMIT_BYTES = 58 * 1024 * 1024


def _rms(x, eps):
    return x * lax.rsqrt(jnp.mean(x * x, axis=-1, keepdims=True) + eps)


def _softplus(x):
    return jnp.maximum(x, 0.0) + jnp.log1p(jnp.exp(-jnp.abs(x)))


def _sigmoid(x):
    return 1.0 / (1.0 + jnp.exp(-x))


def _exact_dot(x, const_bf16):
    hi, lo = _split2(x)
    dot = lambda p: jnp.dot(p, const_bf16, preferred_element_type=F32)
    return dot(hi) + dot(lo)


def _head_indicator(d, n):
    shift = n.bit_length() - 1
    assert 1 << shift == n
    ch = lax.shift_right_logical(lax.broadcasted_iota(jnp.int32, (d, 128), 0), shift)
    hd = lax.broadcasted_iota(jnp.int32, (d, 128), 1)
    cht = lax.shift_right_logical(lax.broadcasted_iota(jnp.int32, (128, d), 1), shift)
    hdt = lax.broadcasted_iota(jnp.int32, (128, d), 0)
    return (ch == hd).astype(BF16), (cht == hdt).astype(BF16)


def _shifted_rows(buf, carry_rows, cur, shift, taps, tm):
    prev = carry_rows.shape[0]
    pad = buf.shape[0] - tm
    buf[pad - prev:pad, :] = carry_rows
    buf[pad:, :] = cur
    return [buf[pad - j * shift:pad - j * shift + tm, :] for j in range(1, taps + 1)]


def _row_grid_call(kernel_fn, name, G, R, tm, in_specs, out_specs, out_shape, scratch_shapes):
    return pl.pallas_call(
        kernel_fn, grid=(G, R // tm), in_specs=in_specs, out_specs=out_specs, out_shape=out_shape,
        scratch_shapes=scratch_shapes,
        compiler_params=pltpu.CompilerParams(dimension_semantics=("parallel", "arbitrary"),
                                             vmem_limit_bytes=LAYER_VMEM_LIMIT_BYTES),
        name=name)


def _tile_spec(tm, c):
    return pl.BlockSpec((1, tm, c), lambda g, t: (g, t, 0))


def _group_spec(rows, c):
    return pl.BlockSpec((1, rows, c), lambda g, t: (g, 0, 0))


def _layer_spec(layer, *blk):
    return pl.BlockSpec((1,) + blk, lambda g, t: (layer,) + (0,) * len(blk))


def _state_row(R, tm, valid_rows, prev):
    row = valid_rows - prev - (R // tm - 1) * tm
    assert 0 <= row and prev <= tm, "the new conv/shift state must come from the last row tile"
    return row


FFN_COL_CHUNK = 256


def _ffn_kernel(x_ref, nw_ref, wup_ref, cw_ref, wdn_ref, cbuf_ref, fnw_ref,
                o_ref, cout_ref, gbuf, carry, *, tm, shift, d_ff, state_row, final_norm):
    t = pl.program_id(1)
    x = x_ref[0]
    hn = (_rms(x, RMS_EPS) * nw_ref[0]).astype(BF16)
    taps = cw_ref.shape[1] - 1
    prev = taps * shift

    @pl.when(t == 0)
    def _():
        carry[...] = cbuf_ref[0]

    acc = jnp.zeros(x.shape, F32)
    for c in range(d_ff // FFN_COL_CHUNK):
        cols = slice(c * FFN_COL_CHUNK, (c + 1) * FFN_COL_CHUNK)
        ucols = slice(d_ff + c * FFN_COL_CHUNK, d_ff + (c + 1) * FFN_COL_CHUNK)
        g = jnp.dot(hn, wup_ref[0, :, cols], preferred_element_type=F32)
        u = jnp.dot(hn, wup_ref[0, :, ucols], preferred_element_type=F32)
        delayed = _shifted_rows(gbuf.at[c % 2], carry[:, cols], g, shift, taps, tm)
        w = cw_ref[0, :, cols]
        gc = g * w[taps:taps + 1]
        for j in range(1, taps + 1):
            gc = gc + delayed[j - 1] * w[taps - j:taps - j + 1]
        act = (gc * _sigmoid(gc) * u).astype(BF16)
        acc = acc + jnp.dot(act, wdn_ref[0, cols, :], preferred_element_type=F32)
        carry[:, cols] = g[tm - prev:, :]
        cout_ref[0, :, cols] = g[state_row:state_row + prev, :]
    y = x + acc
    if final_norm:
        y = _rms(y, RMS_EPS) * fnw_ref[...]
    o_ref[0] = y


def ffn_layer(x, conv_state, p, layer, lay, final_norm):
    G, R, D = x.shape
    d_ff = p['ffn_conv_w'].shape[-1]
    taps = p['ffn_conv_w'].shape[1] - 1
    shift = lay['shift']
    prev = taps * shift
    tm = _pick_tile(R, ROW_TILES)
    pad = -(-prev // 8) * 8
    kern = functools.partial(_ffn_kernel, tm=tm, shift=shift, d_ff=d_ff,
                             state_row=_state_row(R, tm, lay['valid'], prev), final_norm=final_norm)
    return _row_grid_call(
        kern, "ffn_layer", G, R, tm,
        in_specs=[_tile_spec(tm, D), _layer_spec(layer, 1, D), _layer_spec(layer, D, 2 * d_ff),
                  _layer_spec(layer, taps + 1, d_ff), _layer_spec(layer, d_ff, D), _group_spec(prev, d_ff),
                  pl.BlockSpec((1, D), lambda g, t: (0, 0))],
        out_specs=[_tile_spec(tm, D), _group_spec(prev, d_ff)],
        out_shape=[jax.ShapeDtypeStruct((G, R, D), F32), jax.ShapeDtypeStruct((G, prev, d_ff), F32)],
        scratch_shapes=[pltpu.VMEM((2, pad + tm, FFN_COL_CHUNK), F32), pltpu.VMEM((prev, d_ff), F32)],
    )(x, p['norm_ffn3'], p['ffn_w_up'], p['ffn_conv_w'], p['ffn_w_down'], conv_state,
      p['norm_final'].reshape(1, D))


GDN_COL_CHUNK = 256


def _gdn_pre_kernel(x_ref, nw_ref, w_ref, wab_ref, cw_ref, al_ref, dt_ref, cbuf_ref,
                    qkv_ref, z_ref, g_ref, beta_ref, cout_ref, gbuf, carry,
                    *, tm, shift, conv_dim, state_row, valid_rows, heads):
    t = pl.program_id(1)
    hn = (_rms(x_ref[0], RMS_EPS) * nw_ref[0]).astype(BF16)
    taps = cw_ref.shape[1] - 1
    prev = taps * shift

    @pl.when(t == 0)
    def _():
        carry[...] = cbuf_ref[0]

    qk_dim = heads * GDN_DK
    for c in range(conv_dim // GDN_COL_CHUNK):
        cols = slice(c * GDN_COL_CHUNK, (c + 1) * GDN_COL_CHUNK)
        pre = jnp.dot(hn, w_ref[0, :, cols], preferred_element_type=F32)
        delayed = _shifted_rows(gbuf.at[c % 2], carry[:, cols], pre, shift, taps, tm)
        w = cw_ref[0, :, cols]
        acc = pre * w[taps:taps + 1]
        for j in range(1, taps + 1):
            acc = acc + delayed[j - 1] * w[taps - j:taps - j + 1]
        act = acc * _sigmoid(acc)
        carry[:, cols] = pre[tm - prev:, :]
        cout_ref[0, :, cols] = pre[state_row:state_row + prev, :]
        if c * GDN_COL_CHUNK < 2 * qk_dim:
            scale = GDN_DK ** -0.5 if c * GDN_COL_CHUNK < qk_dim else 1.0
            for h in range(GDN_COL_CHUNK // GDN_DK):
                a_h = act[:, h * GDN_DK:(h + 1) * GDN_DK]
                n_h = a_h * lax.rsqrt(jnp.sum(a_h * a_h, axis=-1, keepdims=True) + L2_EPS)
                lo = c * GDN_COL_CHUNK + h * GDN_DK
                qkv_ref[0, :, lo:lo + GDN_DK] = n_h * scale if scale != 1.0 else n_h
        else:
            qkv_ref[0, :, cols] = act
    z_dim = z_ref.shape[-1]
    for c in range(z_dim // GDN_COL_CHUNK):
        cols = slice(c * GDN_COL_CHUNK, (c + 1) * GDN_COL_CHUNK)
        wcols = slice(conv_dim + c * GDN_COL_CHUNK, conv_dim + (c + 1) * GDN_COL_CHUNK)
        z = jnp.dot(hn, w_ref[0, :, wcols], preferred_element_type=F32)
        z_ref[0, :, cols] = z * _sigmoid(z)
    ab = jnp.dot(hn, wab_ref[0], preferred_element_type=F32)
    row = t * tm + lax.broadcasted_iota(jnp.int32, (tm, 1), 0)
    m = (row < valid_rows).astype(F32)
    g_ref[0] = -jnp.exp(al_ref[0]) * _softplus(ab[:, :heads] + dt_ref[0]) * m
    beta_ref[0] = _sigmoid(ab[:, heads:2 * heads]) * m


def gdn_pre(x, conv_state, p, i, layer, lay):
    G, R, D = x.shape
    H = GDN_HEADS
    conv_dim = p['gdn_conv_w'].shape[-1]
    z_dim = p['gdn_w_main'].shape[-1] - conv_dim
    taps = p['gdn_conv_w'].shape[1] - 1
    shift = lay['shift']
    prev = taps * shift
    tm = _pick_tile(R, ROW_TILES)
    pad = -(-prev // 8) * 8
    kern = functools.partial(_gdn_pre_kernel, tm=tm, shift=shift, conv_dim=conv_dim,
                             state_row=_state_row(R, tm, lay['valid'], prev), valid_rows=lay['valid'], heads=H)
    return _row_grid_call(
        kern, "gdn_pre", G, R, tm,
        in_specs=[_tile_spec(tm, D), _layer_spec(i, 1, D), _layer_spec(layer, D, conv_dim + z_dim),
                  _layer_spec(layer, D, 128), _layer_spec(layer, taps + 1, conv_dim),
                  _layer_spec(layer, 1, H), _layer_spec(layer, 1, H), _group_spec(prev, conv_dim)],
        out_specs=[_tile_spec(tm, conv_dim), _tile_spec(tm, z_dim), _tile_spec(tm, H), _tile_spec(tm, H),
                   _group_spec(prev, conv_dim)],
        out_shape=[jax.ShapeDtypeStruct((G, R, conv_dim), F32), jax.ShapeDtypeStruct((G, R, z_dim), F32),
                   jax.ShapeDtypeStruct((G, R, H), F32), jax.ShapeDtypeStruct((G, R, H), F32),
                   jax.ShapeDtypeStruct((G, prev, conv_dim), F32)],
        scratch_shapes=[pltpu.VMEM((2, pad + tm, GDN_COL_CHUNK), F32), pltpu.VMEM((prev, conv_dim), F32)],
    )(x, p['norm_mix3'], p['gdn_w_main'], p['gdn_w_ab'], p['gdn_conv_w'], p['gdn_A_log3'], p['gdn_dt_bias3'], conv_state)


def _gdn_post_kernel(x_ref, o_ref, z_ref, nw_ref, w_ref, y_ref, *, heads):
    o = o_ref[0]
    parts = []
    for h in range(heads):
        o_h = o[:, h * GDN_DV:(h + 1) * GDN_DV]
        parts.append(_rms(o_h, RMS_EPS) * nw_ref[0])
    gated = (jnp.concatenate(parts, axis=1) * z_ref[0]).astype(BF16)
    y_ref[0] = x_ref[0] + jnp.dot(gated, w_ref[0], preferred_element_type=F32)


def gdn_post(x, o, z_act, p, layer):
    G, R, D = x.shape
    v_dim = o.shape[-1]
    tm = _pick_tile(R, ROW_TILES)
    return _row_grid_call(
        functools.partial(_gdn_post_kernel, heads=GDN_HEADS), "gdn_post", G, R, tm,
        in_specs=[_tile_spec(tm, D), _tile_spec(tm, v_dim), _tile_spec(tm, v_dim),
                  _layer_spec(layer, 1, GDN_DV), _layer_spec(layer, v_dim, D)],
        out_specs=_tile_spec(tm, D),
        out_shape=jax.ShapeDtypeStruct((G, R, D), F32),
        scratch_shapes=[],
    )(x, o, z_act, p['gdn_norm_w3'], p['gdn_w_out'])


RWKV_ROW_TILES = (352, 256, 128, 64, 32, 16, 8)


def _rwkv_pre_kernel(*refs, tm, shift, state_row, valid_rows, has_vres):
    (x_ref, nw_ref, mix_ref, wr_ref, wk_ref, wv_ref, w1_ref, w2_ref, a1_ref, a2_ref, g1_ref, g2_ref,
     w0_ref, a0_ref, kk_ref, ka_ref, rk_ref, sh_ref) = refs[:18]
    refs = refs[18:]
    if has_vres:
        v1_ref, v2_ref, v0_ref, vf_ref = refs[:4]
        refs = refs[4:]
    (r_ref, lw_ref, kf_ref, v_ref, kkn_ref, b_ref, gate_ref, bonus_ref, shout_ref, hbuf, carry) = refs
    t = pl.program_id(1)

    @pl.when(t == 0)
    def _():
        carry[...] = sh_ref[0]

    hn = _rms(x_ref[0], RMS_EPS) * nw_ref[0]
    prev = _shifted_rows(hbuf, carry[...], hn, shift, 1, tm)[0]
    carry[...] = hn[tm - shift:, :]
    shout_ref[0] = hn[state_row:state_row + shift, :]
    xx = prev - hn
    mixed = lambda i: (hn + xx * mix_ref[0, i:i + 1, :]).astype(BF16)
    dot = lambda a, w_ref: jnp.dot(a, w_ref[0], preferred_element_type=F32)
    xr, xw, xk, xv, xa, xg = (mixed(i) for i in range(6))
    r = dot(xr, wr_ref)
    k = dot(xk, wk_ref)
    v = dot(xv, wv_ref)
    w_lora = dot(jnp.tanh(dot(xw, w1_ref)).astype(BF16), w2_ref)
    w_log = -_softplus(-(w0_ref[0] + w_lora)) - 0.5
    if has_vres:
        v_lora = dot(dot(xv, v1_ref).astype(BF16), v2_ref)
        v = v + (vf_ref[0] - v) * _sigmoid(v0_ref[0] + v_lora)
    a = _sigmoid(a0_ref[0] + dot(dot(xa, a1_ref).astype(BF16), a2_ref))
    gate_ref[0] = dot(_sigmoid(dot(xg, g1_ref)).astype(BF16), g2_ref)

    D = hn.shape[-1]
    seg, seg_t = _head_indicator(D, RWKV_N)
    kkk = k * kk_ref[0]
    rs = lax.rsqrt(_exact_dot(kkk * kkk, seg) + L2_EPS)
    kkn = kkk * _exact_dot(rs, seg_t)
    kf = k * (1.0 + (a - 1.0) * ka_ref[0])
    bonus_ref[0] = _exact_dot(r * kf * rk_ref[0], seg)

    row = t * tm + lax.broadcasted_iota(jnp.int32, (tm, 1), 0)
    m = (row < valid_rows).astype(F32)
    kkn = kkn * m
    r_ref[0] = r
    lw_ref[0] = -jnp.exp(w_log) * m
    kf_ref[0] = kf * m
    v_ref[0] = v * m
    kkn_ref[0] = kkn
    b_ref[0] = kkn * a


def rwkv_pre(x, shift_state, v_first, p, i, layer, lay):
    G, R, D = x.shape
    s = lay['shift']
    tm = _pick_tile(R, RWKV_ROW_TILES)
    pad = -(-s // 8) * 8
    has_vres = layer > 0
    L = lambda *blk: _layer_spec(layer, *blk)
    lora = lambda name: p[name].shape[-1]
    in_specs = [_tile_spec(tm, D), _layer_spec(i, 1, D), L(6, D), L(D, D), L(D, D), L(D, D),
                L(D, lora('rwkv_w1')), L(lora('rwkv_w1'), D), L(D, lora('rwkv_a1')), L(lora('rwkv_a1'), D),
                L(D, lora('rwkv_g1')), L(lora('rwkv_g1'), D), L(1, D), L(1, D), L(1, D), L(1, D), L(1, D),
                _group_spec(s, D)]
    args = [x, p['norm_mix3'], p['rwkv_mix'], p['rwkv_wr'], p['rwkv_wk'], p['rwkv_wv'], p['rwkv_w1'],
            p['rwkv_w2'], p['rwkv_a1'], p['rwkv_a2'], p['rwkv_g1'], p['rwkv_g2'], p['rwkv_w03'], p['rwkv_a03'],
            p['rwkv_k_k3'], p['rwkv_k_a3'], p['rwkv_r_k3'], shift_state]
    if has_vres:
        V = lambda *blk: _layer_spec(layer - 1, *blk)
        in_specs += [V(D, lora('rwkv_v1')), V(lora('rwkv_v1'), D), V(1, D), _tile_spec(tm, D)]
        args += [p['rwkv_v1'], p['rwkv_v2'], p['rwkv_v03'], v_first]
    act = jax.ShapeDtypeStruct((G, R, D), F32)
    kern = functools.partial(_rwkv_pre_kernel, tm=tm, shift=s, state_row=_state_row(R, tm, lay['valid'], s),
                             valid_rows=lay['valid'], has_vres=has_vres)
    return _row_grid_call(
        kern, "rwkv_pre", G, R, tm, in_specs=in_specs,
        out_specs=[_tile_spec(tm, D)] * 7 + [_tile_spec(tm, 128), _group_spec(s, D)],
        out_shape=[act] * 7 + [jax.ShapeDtypeStruct((G, R, 128), F32), jax.ShapeDtypeStruct((G, s, D), F32)],
        scratch_shapes=[pltpu.VMEM((pad + tm, D), F32), pltpu.VMEM((s, D), F32)],
    )(*args)


def _rwkv_post_kernel(x_ref, y_ref, v_ref, gate_ref, bonus_ref, lnw_ref, lnb_ref, wo_ref, o_ref):
    y = y_ref[0]
    D = y.shape[-1]
    seg, seg_t = _head_indicator(D, RWKV_N)
    inv_n = 1.0 / RWKV_N
    mu = _exact_dot(_exact_dot(y, seg) * inv_n, seg_t)
    d = y - mu
    rs = lax.rsqrt(_exact_dot(d * d, seg) * inv_n + GN_EPS)
    yn = d * _exact_dot(rs, seg_t) * lnw_ref[0] + lnb_ref[0]
    yn = yn + _exact_dot(bonus_ref[0], seg_t) * v_ref[0]
    o_ref[0] = x_ref[0] + jnp.dot((yn * gate_ref[0]).astype(BF16), wo_ref[0], preferred_element_type=F32)


def rwkv_post(x, y, v, gate, bonus, p, layer):
    G, R, D = x.shape
    tm = _pick_tile(R, RWKV_ROW_TILES)
    L = lambda *blk: _layer_spec(layer, *blk)
    return _row_grid_call(
        _rwkv_post_kernel, "rwkv_post", G, R, tm,
        in_specs=[_tile_spec(tm, D)] * 4 + [_tile_spec(tm, 128), L(1, D), L(1, D), L(D, D)],
        out_specs=_tile_spec(tm, D),
        out_shape=jax.ShapeDtypeStruct((G, R, D), F32),
        scratch_shapes=[],
    )(x, y, v, gate, bonus, p['rwkv_ln_w3'], p['rwkv_ln_b3'], p['rwkv_wo'])


def _to_seq(a, lay):
    if lay['shift'] == 1:
        return a
    B = lay['shift']
    T = a.shape[1] // B
    a = a.reshape(T, B, a.shape[-1]).transpose(1, 0, 2)
    return jnp.pad(a, ((0, 0), (0, lay['chunk'] - T), (0, 0)))


def _from_seq(o, lay):
    if lay['shift'] == 1:
        return o
    B = lay['shift']
    T = lay['valid'] // B
    return o[:, :T].transpose(1, 0, 2).reshape(1, T * B, o.shape[-1])


def _gdn_mixer(x, conv_buf, S0, earlier, p, i, j, lay):
    qkv_n, z_act, g, beta, new_buf = gdn_pre(x, conv_buf, p, i, j, lay)
    o, S = gdn_recurrence(_to_seq(qkv_n, lay), _to_seq(g, lay), _to_seq(beta, lay), S0, earlier,
                          lay['chunk'], lay['nseq_gdn'])
    return gdn_post(x, _from_seq(o, lay), z_act, p, j), new_buf, S


def _rwkv_mixer(x, shift, S0, earlier, v_first, p, i, j, lay):
    r, lw, kf, v, kk, b, gate, bonus, new_shift = rwkv_pre(x, shift, v_first, p, i, j, lay)
    seq = lambda t: _to_seq(t, lay)
    y, S = rwkv_recurrence(seq(r), seq(lw), seq(kf), seq(v), seq(kk), seq(b), S0, earlier,
                           lay['chunk'], lay['nseq_rwkv'])
    if j == 0:
        v_first = v
    return rwkv_post(x, _from_seq(y, lay), v, gate, bonus, p, j), new_shift, S, v_first


def _trunk(x, gdn_S, gdn_conv, rwkv_S, rwkv_shift, ffn_conv, p, lay):
    depth = p['depth']
    n_gdn, n_rwkv = gdn_S.shape[0], rwkv_S.shape[0]
    gS, gC, rS, rSh, fC = [], [], [], [], []
    v_first = None
    for i in range(depth):
        j = i // 2
        if i % 2 == 0:
            x, cb, S = _gdn_mixer(x, gdn_conv[j], gdn_S[j], gS if j == n_gdn - 1 else [], p, i, j, lay)
            gS = S if j == n_gdn - 1 else gS + [S[0]]
            gC.append(cb)
        else:
            x, sh, S, v_first = _rwkv_mixer(x, rwkv_shift[j], rwkv_S[j], rS if j == n_rwkv - 1 else [], v_first,
                                            p, i, j, lay)
            rS = S if j == n_rwkv - 1 else rS + [S[0]]
            rSh.append(sh)
        x, fb = ffn_layer(x, ffn_conv[i], p, i, lay, final_norm=(i == depth - 1))
        fC.append(fb)
    return x, gS, jnp.stack(gC), rS, jnp.stack(rSh), jnp.stack(fC)


def _pad_tokens(x, chunk):
    L = x.shape[1]
    Lp = -(-L // chunk) * chunk
    return jnp.pad(x, ((0, 0), (0, Lp - L), (0, 0)))


def _rows_state(s):
    N, B, W, C = s.shape
    return s.transpose(0, 2, 1, 3).reshape(N, 1, W * B, C)


def _seq_state(s, B):
    N, _, WB, C = s.shape
    return s.reshape(N, WB // B, B, C).transpose(0, 2, 1, 3)


def kernel(x_prompt, x_sample, state_gdn_S, state_gdn_conv, state_rwkv_S, state_rwkv_shift, state_ffn_conv, meta_tokens, norm_mix, norm_ffn, norm_final, gdn_w_in, gdn_conv_w, gdn_A_log, gdn_dt_bias, gdn_norm_w, gdn_w_out, rwkv_mix, rwkv_wr, rwkv_wk, rwkv_wv, rwkv_wo, rwkv_w0, rwkv_w1, rwkv_w2, rwkv_a0, rwkv_a1, rwkv_a2, rwkv_g1, rwkv_g2, rwkv_k_k, rwkv_k_a, rwkv_r_k, rwkv_ln_w, rwkv_ln_b, rwkv_v0, rwkv_v1, rwkv_v2, ffn_w_up, ffn_conv_w, ffn_w_down):
    n_main = gdn_w_in.shape[-1] - 2 * GDN_HEADS
    w_ab = jnp.pad(gdn_w_in[..., n_main:], ((0, 0), (0, 0), (0, 128 - 2 * GDN_HEADS)))
    bf = lambda w: w.astype(BF16)
    three = lambda a: a.reshape(a.shape[0], 1, a.shape[-1])
    p = dict(depth=norm_mix.shape[0], norm_mix3=three(norm_mix), norm_ffn3=three(norm_ffn), norm_final=norm_final,
             gdn_w_main=bf(gdn_w_in[..., :n_main]), gdn_w_ab=bf(w_ab), gdn_conv_w=gdn_conv_w,
             gdn_A_log3=three(gdn_A_log), gdn_dt_bias3=three(gdn_dt_bias), gdn_norm_w3=three(gdn_norm_w),
             gdn_w_out=bf(gdn_w_out), rwkv_mix=rwkv_mix, rwkv_wr=bf(rwkv_wr), rwkv_wk=bf(rwkv_wk),
             rwkv_wv=bf(rwkv_wv), rwkv_wo=bf(rwkv_wo), rwkv_w1=bf(rwkv_w1), rwkv_w2=bf(rwkv_w2),
             rwkv_a1=bf(rwkv_a1), rwkv_a2=bf(rwkv_a2), rwkv_g1=bf(rwkv_g1), rwkv_g2=bf(rwkv_g2),
             rwkv_v1=bf(rwkv_v1), rwkv_v2=bf(rwkv_v2), rwkv_w03=three(rwkv_w0), rwkv_a03=three(rwkv_a0),
             rwkv_v03=three(rwkv_v0), rwkv_k_k3=three(rwkv_k_k), rwkv_k_a3=three(rwkv_k_a),
             rwkv_r_k3=rwkv_r_k.reshape(rwkv_r_k.shape[0], 1, -1), rwkv_ln_w3=three(rwkv_ln_w),
             rwkv_ln_b3=three(rwkv_ln_b), ffn_w_up=bf(ffn_w_up), ffn_conv_w=ffn_conv_w, ffn_w_down=bf(ffn_w_down))

    B = x_prompt.shape[0]
    D = x_prompt.shape[-1]
    meta = jnp.broadcast_to(meta_tokens[None], (B, N_META, D))
    xp = jnp.concatenate([meta, x_prompt], axis=1)
    Lp = xp.shape[1]
    zeros_like_batch = lambda s: jnp.zeros((s.shape[0], B) + s.shape[2:], F32)
    yp, gS_p, gC_p, rS_p, rSh_p, fC_p = _trunk(
        _pad_tokens(xp, PROMPT_CHUNK), zeros_like_batch(state_gdn_S), zeros_like_batch(state_gdn_conv),
        zeros_like_batch(state_rwkv_S), zeros_like_batch(state_rwkv_shift)[:, :, None],
        zeros_like_batch(state_ffn_conv), p, dict(shift=1, valid=Lp, chunk=PROMPT_CHUNK, nseq_gdn=_pick_tile(B, (PROMPT_GDN_SEQS_PER_STEP, 2, 1)),
             nseq_rwkv=_pick_tile(B, (PROMPT_RWKV_SEQS_PER_STEP, 1))))
    y_prompt = yp[:, N_META:Lp]
    rSh_p = rSh_p[:, :, 0]

    Bs, Ls, _ = x_sample.shape
    xs = x_sample.transpose(1, 0, 2).reshape(1, Ls * Bs, D)
    ys, gS_s, gC_s, rS_s, rSh_s, fC_s = _trunk(
        xs, state_gdn_S, _rows_state(state_gdn_conv), state_rwkv_S, state_rwkv_shift[:, None],
        _rows_state(state_ffn_conv), p, dict(shift=Bs, valid=Ls * Bs, chunk=SAMPLE_CHUNK, nseq_gdn=_pick_tile(Bs, (SAMPLE_SEQS_PER_STEP, 4, 2, 1)),
             nseq_rwkv=_pick_tile(Bs, (SAMPLE_SEQS_PER_STEP, 4, 2, 1))))
    y_sample = ys.reshape(Ls, Bs, D).transpose(1, 0, 2)
    gC_s = _seq_state(gC_s, Bs)
    fC_s = _seq_state(fC_s, Bs)
    rSh_s = rSh_s[:, 0]
    return (y_prompt, y_sample, gS_p, gC_p, rS_p, rSh_p, fC_p, gS_s, gC_s, rS_s, rSh_s, fC_s)
```

```python
import functools

import jax
import jax.numpy as jnp
from jax import lax
from jax.experimental import pallas as pl
from jax.experimental.pallas import tpu as pltpu

F32 = jnp.float32
BF16 = jnp.bfloat16

N_META = 16
GDN_HEADS = 8
GDN_DK = 128
GDN_DV = 128
RWKV_N = 64
RMS_EPS = 1e-6
L2_EPS = 1e-6
GN_EPS = 64e-5

PROMPT_CHUNK = 64
SAMPLE_CHUNK = 8
SAMPLE_SEQS_PER_STEP = 8
PROMPT_GDN_SEQS_PER_STEP = 4
PROMPT_RWKV_SEQS_PER_STEP = 2
SOLVE_SPLIT_SPAN = 8
VMEM_LIMIT_BYTES = 48 * 1024 * 1024


def _pick_tile(n, cands):
    for c in cands:
        if n % c == 0:
            return c
    return n


def _dot(a, b):
    return jnp.dot(a.astype(BF16), b.astype(BF16), preferred_element_type=F32)


def _dot_nt(a, b):
    return lax.dot_general(a.astype(BF16), b.astype(BF16), (((1,), (1,)), ((), ())),
                           preferred_element_type=F32)


def _dot_tn(a, b):
    return lax.dot_general(a.astype(BF16), b.astype(BF16), (((0,), (0,)), ((), ())),
                           preferred_element_type=F32)


def _split2(x):
    hi = x.astype(BF16)
    return hi, (x - hi.astype(F32)).astype(BF16)


def _dot3(a, b):
    ah, al = _split2(a)
    bh, bl = _split2(b)
    d = lambda x, y: jnp.dot(x, y, preferred_element_type=F32)
    return d(ah, bh) + (d(ah, bl) + d(al, bh))


LANES = 128
HALF_LANES = 64


def _stacked_state_spec(layers, nseq, state_shape):
    return pl.BlockSpec((layers, nseq) + tuple(state_shape), lambda b, c: (0, b) + (0,) * len(state_shape))


def _write_stacked_states(sout_ref, earlier_refs, s_scr):
    for i, ref in enumerate(earlier_refs):
        sout_ref[i] = ref[...]
    sout_ref[len(earlier_refs)] = s_scr[...]


def _pad_lanes(a):
    return jnp.concatenate([a, jnp.zeros(a.shape[:-1] + (LANES - a.shape[-1],), a.dtype)], axis=-1)


def _pad_lanes_left(a):
    return jnp.concatenate([jnp.zeros(a.shape[:-1] + (LANES - a.shape[-1],), a.dtype), a], axis=-1)


def _cumsum_rows(incl_bf16, x):
    hi = x.astype(BF16)
    r1 = x - hi.astype(F32)
    mid = r1.astype(BF16)
    lo = (r1 - mid.astype(F32)).astype(BF16)
    dot = lambda p: jnp.dot(incl_bf16, p, preferred_element_type=F32)
    return dot(hi) + dot(mid) + dot(lo)


def _gdn_kernel(q_ref, k_ref, v_ref, gcol_ref, bcol_ref, grow_ref, s0_ref, *rest, chunk, heads, nseq):
    earlier_refs, (o_ref, sout_ref, s_scr) = rest[:-3], rest[-3:]
    c = pl.program_id(1)

    @pl.when(c == 0)
    def _():
        s_scr[...] = s0_ref[...]

    ri = lax.broadcasted_iota(jnp.int32, (chunk, chunk), 0)
    ci = lax.broadcasted_iota(jnp.int32, (chunk, chunk), 1)
    causal = ri >= ci
    strict = ri > ci
    units = [(s, h) for s in range(nseq) for h in range(heads)]
    us = range(len(units))
    sl = [slice(h * GDN_DK, (h + 1) * GDN_DK) for _, h in units]
    q = [q_ref[s, :, sl[u]] for u, (s, h) in enumerate(units)]
    k = [k_ref[s, :, sl[u]] for u, (s, h) in enumerate(units)]
    v = [v_ref[s, :, sl[u]] for u, (s, h) in enumerate(units)]
    g_col = [gcol_ref[s, :, h:h + 1] for s, h in units]
    beta = [bcol_ref[s, :, h:h + 1] for s, h in units]
    g_row = [grow_ref[s, 0, h:h + 1, :] for s, h in units]
    G_col = [jnp.sum(jnp.where(causal, g_row[u], 0.0), axis=1, keepdims=True) for u in us]
    G_row = [jnp.sum(jnp.where(ri <= ci, g_col[u], 0.0), axis=0, keepdims=True) for u in us]
    g_last = [jnp.sum(g_row[u], axis=1, keepdims=True) for u in us]
    decay = [jnp.where(causal, jnp.exp(jnp.minimum(G_col[u] - G_row[u], 0.0)), 0.0) for u in us]
    eG = [jnp.exp(G_col[u]) for u in us]
    k_bf = [k[u].astype(BF16) for u in us]
    kb = [k[u] * beta[u] for u in us]
    S = [s_scr[s, h] for s, h in units]
    S_bf = [S[u].astype(BF16) for u in us]
    n_strict = [jnp.where(strict, _dot_nt(kb[u], k_bf[u]) * decay[u], 0.0) for u in us]
    qk = [(_dot_nt(q[u], k_bf[u]) * decay[u]).astype(BF16) for u in us]
    qS = [_dot(q[u] * eG[u], S_bf[u]) for u in us]
    lane = lax.broadcasted_iota(jnp.int32, (chunk, LANES), 1)
    row = lax.broadcasted_iota(jnp.int32, (chunk, LANES), 0)
    eye_hi = (lane == row + HALF_LANES).astype(F32)
    is_p = lane < HALF_LANES
    Z = [_pad_lanes(n_strict[u]) + eye_hi for u in us]
    span = 1
    sign = -1.0
    while span < chunk:
        dot = _dot3 if span < SOLVE_SPLIT_SPAN else _dot
        PZ = [dot(Z[u][:, :chunk], Z[u]) for u in us]
        Z = [jnp.where(is_p, PZ[u], Z[u] + sign * PZ[u]) for u in us]
        sign = 1.0
        span *= 2
    rhs = [jnp.concatenate([v[u] * beta[u], kb[u] * eG[u]], axis=1) for u in us]
    x = [_dot3(Z[u][:, HALF_LANES:HALF_LANES + chunk], rhs[u]) for u in us]
    v_new = [x[u][:, :GDN_DV] - _dot(x[u][:, GDN_DV:], S_bf[u]) for u in us]
    v_new_bf = [v_new[u].astype(BF16) for u in us]
    for u, (s, h) in enumerate(units):
        o_ref[s, :, sl[u]] = qS[u] + _dot(qk[u], v_new_bf[u])
    for u, (s, h) in enumerate(units):
        k_dec = k[u] * jnp.exp(g_last[u] - G_col[u])
        s_scr[s, h] = S[u] * jnp.exp(g_last[u]) + _dot_tn(k_dec, v_new_bf[u])

    @pl.when(c == pl.num_programs(1) - 1)
    def _():
        _write_stacked_states(sout_ref, earlier_refs, s_scr)


def gdn_recurrence(qkv, g, beta, S0, earlier, chunk, nseq):
    B, L, _ = qkv.shape
    H = GDN_HEADS
    n = L // chunk
    g_row = jnp.transpose(g, (0, 2, 1)).reshape(B, H, n, chunk).transpose(0, 2, 1, 3)
    blk = lambda part: pl.BlockSpec((nseq, chunk, H * GDN_DK), lambda b, c, part=part: (b, c, part))
    col = pl.BlockSpec((nseq, chunk, H), lambda b, c: (b, c, 0))
    st = pl.BlockSpec((nseq, H, GDN_DK, GDN_DV), lambda b, c: (b, 0, 0, 0))
    o, S = pl.pallas_call(
        functools.partial(_gdn_kernel, chunk=chunk, heads=H, nseq=nseq),
        grid=(B // nseq, n),
        in_specs=[blk(0), blk(1), blk(2), col, col,
                  pl.BlockSpec((nseq, 1, H, chunk), lambda b, c: (b, c, 0, 0)), st] + [st] * len(earlier),
        out_specs=[blk(0), _stacked_state_spec(len(earlier) + 1, nseq, S0.shape[1:])],
        out_shape=[jax.ShapeDtypeStruct((B, L, H * GDN_DV), F32),
                   jax.ShapeDtypeStruct((len(earlier) + 1,) + S0.shape, F32)],
        scratch_shapes=[pltpu.VMEM((nseq, H, GDN_DK, GDN_DV), F32)],
        compiler_params=pltpu.CompilerParams(
            dimension_semantics=("parallel", "arbitrary"),
            vmem_limit_bytes=VMEM_LIMIT_BYTES),
        name="gdn_recurrence",
    )(qkv, qkv, qkv, g, beta, g_row, S0, *earlier)
    return o, S


def _rwkv_kernel(r_ref, lw_ref, k_ref, v_ref, kk_ref, b_ref, s0_ref, *rest, chunk, heads, nseq):
    earlier_refs, (o_ref, sout_ref, s_scr) = rest[:-3], rest[-3:]
    c = pl.program_id(1)

    @pl.when(c == 0)
    def _():
        s_scr[...] = s0_ref[...]

    ri = lax.broadcasted_iota(jnp.int32, (chunk, chunk), 0)
    ci = lax.broadcasted_iota(jnp.int32, (chunk, chunk), 1)
    incl = (ri >= ci).astype(BF16)
    ri2 = lax.broadcasted_iota(jnp.int32, (2 * chunk, 2 * chunk), 0)
    ci2 = lax.broadcasted_iota(jnp.int32, (2 * chunk, 2 * chunk), 1)
    cj = jnp.where(ci2 >= chunk, ci2 - chunk, ci2)
    keep = ((ri2 < chunk) & (ri2 > cj)) | ((ri2 >= chunk) & ((ri2 - chunk) >= cj))

    rows_k, rows_r, cols_k, cols_b, k_dec, b_dec, v_bf, P_last = [], [], [], [], [], [], [], []
    for s in range(nseq):
        lw = lw_ref[s]
        LP = _cumsum_rows(incl, lw)
        LP_last = LP[chunk - 1:chunk, :]
        P_inv = jnp.exp(-LP)
        dec = jnp.exp(LP_last - LP)
        P_last.append(jnp.exp(LP_last))
        kk = kk_ref[s]
        b = b_ref[s]
        k = k_ref[s]
        rows_k.append((kk * jnp.exp(LP - lw)).astype(BF16))
        rows_r.append((r_ref[s] * jnp.exp(LP)).astype(BF16))
        cols_k.append((k * P_inv).astype(BF16))
        cols_b.append((b * P_inv).astype(BF16))
        k_dec.append((k * dec).astype(BF16))
        b_dec.append((b * dec).astype(BF16))
        v_bf.append(v_ref[s].astype(BF16))

    units = [(s, h) for s in range(nseq) for h in range(heads)]
    us = range(len(units))
    sl = [slice(h * RWKV_N, (h + 1) * RWKV_N) for _, h in units]
    rows = [jnp.concatenate([rows_k[s][:, sl[u]], rows_r[s][:, sl[u]]], axis=0) for u, (s, h) in enumerate(units)]
    cols = [jnp.concatenate([cols_k[s][:, sl[u]], cols_b[s][:, sl[u]]], axis=0) for u, (s, h) in enumerate(units)]
    S = [s_scr[s, h] for s, h in units]
    A = [jnp.where(keep, _dot_nt(rows[u], cols[u]), 0.0).astype(BF16) for u in us]
    base = [_dot_nt(rows[u], S[u]) for u in us]
    vh = [v_bf[s][:, sl[u]] for u, (s, h) in enumerate(units)]
    lane = lax.broadcasted_iota(jnp.int32, (chunk, LANES), 1)
    is_p = lane < HALF_LANES
    zero_v = jnp.zeros((chunk, HALF_LANES), BF16)
    Av = [_dot(A[u][:, :chunk], jnp.concatenate([zero_v, vh[u]], axis=1)) for u in us]
    Z = [jnp.where(is_p, _pad_lanes(A[u][:chunk, chunk:]).astype(F32),
                   _pad_lanes_left(base[u][:chunk]) + Av[u][:chunk]) for u in us]
    span = 1
    sign = -1.0
    while span < chunk:
        PZ = [_dot(Z[u][:, :chunk], Z[u]) for u in us]
        Z = [jnp.where(is_p, PZ[u], Z[u] + sign * PZ[u]) for u in us]
        sign = 1.0
        span *= 2
    sol = [Z[u][:, HALF_LANES:].astype(BF16) for u in us]
    for u, (s, h) in enumerate(units):
        o_ref[s, :, sl[u]] = (base[u][chunk:] + Av[u][chunk:, HALF_LANES:]
                              - _dot(A[u][chunk:, chunk:], sol[u]))
    for u, (s, h) in enumerate(units):
        s_scr[s, h] = (S[u] * P_last[s][:, sl[u]] + _dot_tn(vh[u], k_dec[s][:, sl[u]])
                       - _dot_tn(sol[u], b_dec[s][:, sl[u]]))

    @pl.when(c == pl.num_programs(1) - 1)
    def _():
        _write_stacked_states(sout_ref, earlier_refs, s_scr)


def rwkv_recurrence(r, lw, k, v, kk, b, S0, earlier, chunk, nseq):
    B, L, D = r.shape
    n = L // chunk
    heads = D // RWKV_N
    blk = pl.BlockSpec((nseq, chunk, D), lambda bb, c: (bb, c, 0))
    st = pl.BlockSpec((nseq, heads, RWKV_N, RWKV_N), lambda bb, c: (bb, 0, 0, 0))
    o, S = pl.pallas_call(
        functools.partial(_rwkv_kernel, chunk=chunk, heads=heads, nseq=nseq),
        grid=(B // nseq, n),
        in_specs=[blk] * 6 + [st] * (1 + len(earlier)),
        out_specs=[blk, _stacked_state_spec(len(earlier) + 1, nseq, S0.shape[1:])],
        out_shape=[jax.ShapeDtypeStruct((B, L, D), F32),
                   jax.ShapeDtypeStruct((len(earlier) + 1,) + S0.shape, F32)],
        scratch_shapes=[pltpu.VMEM((nseq, heads, RWKV_N, RWKV_N), F32)],
        compiler_params=pltpu.CompilerParams(
            dimension_semantics=("parallel", "arbitrary"),
            vmem_limit_bytes=VMEM_LIMIT_BYTES),
        name="rwkv_recurrence",
    )(r, lw, k, v, kk, b, S0, *earlier)
    return o, S


ROW_TILES = (528, 512, 256, 128, 64, 32, 16, 8)
LAYER_VMEM_LIMIT_BYTES = 58 * 1024 * 1024


def _rms(x, eps):
    return x * lax.rsqrt(jnp.mean(x * x, axis=-1, keepdims=True) + eps)


def _softplus(x):
    return jnp.maximum(x, 0.0) + jnp.log1p(jnp.exp(-jnp.abs(x)))


def _sigmoid(x):
    return 0.5 + 0.5 * jnp.tanh(0.5 * x)


def _silu(x):
    hx = 0.5 * x
    return hx + hx * jnp.tanh(hx)


def _exact_dot(x, const_bf16):
    hi, lo = _split2(x)
    dot = lambda p: jnp.dot(p, const_bf16, preferred_element_type=F32)
    return dot(hi) + dot(lo)


def _head_indicator(d, n):
    shift = n.bit_length() - 1
    assert 1 << shift == n
    ch = lax.shift_right_logical(lax.broadcasted_iota(jnp.int32, (d, 128), 0), shift)
    hd = lax.broadcasted_iota(jnp.int32, (d, 128), 1)
    cht = lax.shift_right_logical(lax.broadcasted_iota(jnp.int32, (128, d), 1), shift)
    hdt = lax.broadcasted_iota(jnp.int32, (128, d), 0)
    return (ch == hd).astype(BF16), (cht == hdt).astype(BF16)


def _shifted_rows(buf, carry_rows, cur, shift, taps, tm):
    prev = carry_rows.shape[0]
    pad = buf.shape[0] - tm
    buf[pad - prev:pad, :] = carry_rows
    buf[pad:, :] = cur
    return [buf[pad - j * shift:pad - j * shift + tm, :] for j in range(1, taps + 1)]


def _row_grid_call(kernel_fn, name, G, R, tm, in_specs, out_specs, out_shape, scratch_shapes):
    return pl.pallas_call(
        kernel_fn, grid=(G, R // tm), in_specs=in_specs, out_specs=out_specs, out_shape=out_shape,
        scratch_shapes=scratch_shapes,
        compiler_params=pltpu.CompilerParams(dimension_semantics=("parallel", "arbitrary"),
                                             vmem_limit_bytes=LAYER_VMEM_LIMIT_BYTES),
        name=name)


def _tile_spec(tm, c):
    return pl.BlockSpec((1, tm, c), lambda g, t: (g, t, 0))


def _group_spec(rows, c):
    return pl.BlockSpec((1, rows, c), lambda g, t: (g, 0, 0))


def _layer_spec(layer, *blk):
    return pl.BlockSpec((1,) + blk, lambda g, t: (layer,) + (0,) * len(blk))


def _state_row(R, tm, valid_rows, prev):
    row = valid_rows - prev - (R // tm - 1) * tm
    assert 0 <= row and prev <= tm, "the new conv/shift state must come from the last row tile"
    return row


def _gdn_out(o_ref, z_ref, nw_ref, w_ref, *, heads):
    o = o_ref[0]
    parts = [_rms(o[:, h * GDN_DV:(h + 1) * GDN_DV], RMS_EPS) * nw_ref[0] for h in range(heads)]
    gated = (jnp.concatenate(parts, axis=1) * z_ref[0]).astype(BF16)
    return jnp.dot(gated, w_ref[0], preferred_element_type=F32)


FFN_COL_CHUNK = 256


def _ffn_kernel(x_ref, nw_ref, wup_ref, cw_ref, wdn_ref, cbuf_ref, fnw_ref, *rest,
                tm, shift, d_ff, state_row, final_norm, gdn_heads):
    mixer_refs, (o_ref, cout_ref, gbuf, carry) = rest[:-4], rest[-4:]
    t = pl.program_id(1)
    x = x_ref[0]
    if gdn_heads:
        x = x + _gdn_out(*mixer_refs, heads=gdn_heads)
    hn = (_rms(x, RMS_EPS) * nw_ref[0]).astype(BF16)
    taps = cw_ref.shape[1] - 1
    prev = taps * shift

    @pl.when(t == 0)
    def _():
        carry[...] = cbuf_ref[0]

    n_chunks = d_ff // FFN_COL_CHUNK

    def up(c):
        cols = slice(c * FFN_COL_CHUNK, (c + 1) * FFN_COL_CHUNK)
        ucols = slice(d_ff + c * FFN_COL_CHUNK, d_ff + (c + 1) * FFN_COL_CHUNK)
        return (jnp.dot(hn, wup_ref[0, :, cols], preferred_element_type=F32),
                jnp.dot(hn, wup_ref[0, :, ucols], preferred_element_type=F32))

    acc = jnp.zeros(x.shape, F32)
    nxt = up(0)
    for c in range(n_chunks):
        cols = slice(c * FFN_COL_CHUNK, (c + 1) * FFN_COL_CHUNK)
        g, u = nxt
        if c + 1 < n_chunks:
            nxt = up(c + 1)
        delayed = _shifted_rows(gbuf.at[c % 2], carry[:, cols], g, shift, taps, tm)
        w = cw_ref[0, :, cols]
        gc = g * w[taps:taps + 1]
        for j in range(1, taps + 1):
            gc = gc + delayed[j - 1] * w[taps - j:taps - j + 1]
        act = (_silu(gc) * u).astype(BF16)
        acc = acc + jnp.dot(act, wdn_ref[0, cols, :], preferred_element_type=F32)
        carry[:, cols] = g[tm - prev:, :]
        cout_ref[0, :, cols] = g[state_row:state_row + prev, :]
    y = x + acc
    if final_norm:
        y = _rms(y, RMS_EPS) * fnw_ref[...]
    o_ref[0] = y


def ffn_layer(x, conv_state, p, layer, lay, final_norm, gdn_out=None):
    G, R, D = x.shape
    d_ff = p['ffn_conv_w'].shape[-1]
    taps = p['ffn_conv_w'].shape[1] - 1
    shift = lay['shift']
    prev = taps * shift
    tm = _pick_tile(R, ROW_TILES)
    pad = -(-prev // 8) * 8
    mixer_specs, mixer_args = [], []
    if gdn_out is not None:
        o, z_act, j = gdn_out
        v_dim = o.shape[-1]
        mixer_specs = [_tile_spec(tm, v_dim), _tile_spec(tm, v_dim), _layer_spec(j, 1, GDN_DV),
                       _layer_spec(j, v_dim, D)]
        mixer_args = [o, z_act, p['gdn_norm_w3'], p['gdn_w_out']]
    kern = functools.partial(_ffn_kernel, tm=tm, shift=shift, d_ff=d_ff,
                             state_row=_state_row(R, tm, lay['valid'], prev), final_norm=final_norm,
                             gdn_heads=GDN_HEADS if gdn_out is not None else 0)
    return _row_grid_call(
        kern, "ffn_layer", G, R, tm,
        in_specs=[_tile_spec(tm, D), _layer_spec(layer, 1, D), _layer_spec(layer, D, 2 * d_ff),
                  _layer_spec(layer, taps + 1, d_ff), _layer_spec(layer, d_ff, D), _group_spec(prev, d_ff),
                  pl.BlockSpec((1, D), lambda g, t: (0, 0))] + mixer_specs,
        out_specs=[_tile_spec(tm, D), _group_spec(prev, d_ff)],
        out_shape=[jax.ShapeDtypeStruct((G, R, D), F32), jax.ShapeDtypeStruct((G, prev, d_ff), F32)],
        scratch_shapes=[pltpu.VMEM((2, pad + tm, FFN_COL_CHUNK), F32), pltpu.VMEM((prev, d_ff), F32)],
    )(x, p['norm_ffn3'], p['ffn_w_up'], p['ffn_conv_w'], p['ffn_w_down'], conv_state,
      p['norm_final'].reshape(1, D), *mixer_args)


GDN_COL_CHUNK = 256


def _gdn_pre_kernel(x_ref, nw_ref, w_ref, cw_ref, al_ref, dt_ref, cbuf_ref,
                    qkv_ref, z_ref, g_ref, beta_ref, cout_ref, gbuf, carry,
                    *, tm, shift, conv_dim, state_row, valid_rows, heads):
    t = pl.program_id(1)
    hn = (_rms(x_ref[0], RMS_EPS) * nw_ref[0]).astype(BF16)
    taps = cw_ref.shape[1] - 1
    prev = taps * shift

    @pl.when(t == 0)
    def _():
        carry[...] = cbuf_ref[0]

    qk_dim = heads * GDN_DK
    for c in range(conv_dim // GDN_COL_CHUNK):
        cols = slice(c * GDN_COL_CHUNK, (c + 1) * GDN_COL_CHUNK)
        pre = jnp.dot(hn, w_ref[0, :, cols], preferred_element_type=F32)
        delayed = _shifted_rows(gbuf.at[c % 2], carry[:, cols], pre, shift, taps, tm)
        w = cw_ref[0, :, cols]
        acc = pre * w[taps:taps + 1]
        for j in range(1, taps + 1):
            acc = acc + delayed[j - 1] * w[taps - j:taps - j + 1]
        act = _silu(acc)
        carry[:, cols] = pre[tm - prev:, :]
        cout_ref[0, :, cols] = pre[state_row:state_row + prev, :]
        if c * GDN_COL_CHUNK < 2 * qk_dim:
            scale = GDN_DK ** -0.5 if c * GDN_COL_CHUNK < qk_dim else 1.0
            for h in range(GDN_COL_CHUNK // GDN_DK):
                a_h = act[:, h * GDN_DK:(h + 1) * GDN_DK]
                n_h = a_h * lax.rsqrt(jnp.sum(a_h * a_h, axis=-1, keepdims=True) + L2_EPS)
                lo = c * GDN_COL_CHUNK + h * GDN_DK
                qkv_ref[0, :, lo:lo + GDN_DK] = n_h * scale if scale != 1.0 else n_h
        else:
            qkv_ref[0, :, cols] = act
    z_dim = z_ref.shape[-1]
    for c in range(z_dim // GDN_COL_CHUNK):
        cols = slice(c * GDN_COL_CHUNK, (c + 1) * GDN_COL_CHUNK)
        wcols = slice(conv_dim + c * GDN_COL_CHUNK, conv_dim + (c + 1) * GDN_COL_CHUNK)
        z = jnp.dot(hn, w_ref[0, :, wcols], preferred_element_type=F32)
        z_ref[0, :, cols] = _silu(z)
    ab_lo = conv_dim + z_dim
    ab = jnp.dot(hn, w_ref[0, :, ab_lo:ab_lo + 2 * heads], preferred_element_type=F32)
    row = t * tm + lax.broadcasted_iota(jnp.int32, (tm, 1), 0)
    m = (row < valid_rows).astype(F32)
    g_ref[0] = -jnp.exp(al_ref[0]) * _softplus(ab[:, :heads] + dt_ref[0]) * m
    beta_ref[0] = _sigmoid(ab[:, heads:2 * heads]) * m


def gdn_pre(x, conv_state, p, i, layer, lay):
    G, R, D = x.shape
    H = GDN_HEADS
    conv_dim = p['gdn_conv_w'].shape[-1]
    w_cols = p['gdn_w_in'].shape[-1]
    z_dim = w_cols - conv_dim - 2 * H
    taps = p['gdn_conv_w'].shape[1] - 1
    shift = lay['shift']
    prev = taps * shift
    tm = _pick_tile(R, ROW_TILES)
    pad = -(-prev // 8) * 8
    kern = functools.partial(_gdn_pre_kernel, tm=tm, shift=shift, conv_dim=conv_dim,
                             state_row=_state_row(R, tm, lay['valid'], prev), valid_rows=lay['valid'], heads=H)
    return _row_grid_call(
        kern, "gdn_pre", G, R, tm,
        in_specs=[_tile_spec(tm, D), _layer_spec(i, 1, D), _layer_spec(layer, D, w_cols),
                  _layer_spec(layer, taps + 1, conv_dim),
                  _layer_spec(layer, 1, H), _layer_spec(layer, 1, H), _group_spec(prev, conv_dim)],
        out_specs=[_tile_spec(tm, conv_dim), _tile_spec(tm, z_dim), _tile_spec(tm, H), _tile_spec(tm, H),
                   _group_spec(prev, conv_dim)],
        out_shape=[jax.ShapeDtypeStruct((G, R, conv_dim), F32), jax.ShapeDtypeStruct((G, R, z_dim), F32),
                   jax.ShapeDtypeStruct((G, R, H), F32), jax.ShapeDtypeStruct((G, R, H), F32),
                   jax.ShapeDtypeStruct((G, prev, conv_dim), F32)],
        scratch_shapes=[pltpu.VMEM((2, pad + tm, GDN_COL_CHUNK), F32), pltpu.VMEM((prev, conv_dim), F32)],
    )(x, p['norm_mix3'], p['gdn_w_in'], p['gdn_conv_w'], p['gdn_A_log3'], p['gdn_dt_bias3'], conv_state)


RWKV_ROW_TILES = (352, 256, 128, 64, 32, 16, 8)


def _rwkv_pre_kernel(*refs, tm, shift, state_row, valid_rows, has_vres):
    (x_ref, nw_ref, mix_ref, wr_ref, wk_ref, wv_ref, w1_ref, w2_ref, a1_ref, a2_ref, g1_ref, g2_ref,
     w0_ref, a0_ref, kk_ref, ka_ref, rk_ref, sh_ref) = refs[:18]
    refs = refs[18:]
    if has_vres:
        v1_ref, v2_ref, v0_ref, vf_ref = refs[:4]
        refs = refs[4:]
    (r_ref, lw_ref, kf_ref, v_ref, kkn_ref, b_ref, gate_ref, bonus_ref, shout_ref, hbuf, carry) = refs
    t = pl.program_id(1)

    @pl.when(t == 0)
    def _():
        carry[...] = sh_ref[0]

    hn = _rms(x_ref[0], RMS_EPS) * nw_ref[0]
    prev = _shifted_rows(hbuf, carry[...], hn, shift, 1, tm)[0]
    carry[...] = hn[tm - shift:, :]
    shout_ref[0] = hn[state_row:state_row + shift, :]
    xx = prev - hn
    mixed = lambda i: (hn + xx * mix_ref[0, i:i + 1, :]).astype(BF16)
    dot = lambda a, w_ref: jnp.dot(a, w_ref[0], preferred_element_type=F32)
    xr, xw, xk, xv, xa, xg = (mixed(i) for i in range(6))
    r = dot(xr, wr_ref)
    k = dot(xk, wk_ref)
    v = dot(xv, wv_ref)
    w_lora = dot(jnp.tanh(dot(xw, w1_ref)).astype(BF16), w2_ref)
    w_log = -_softplus(-(w0_ref[0] + w_lora)) - 0.5
    if has_vres:
        v_lora = dot(dot(xv, v1_ref).astype(BF16), v2_ref)
        v = v + (vf_ref[0] - v) * _sigmoid(v0_ref[0] + v_lora)
    a = _sigmoid(a0_ref[0] + dot(dot(xa, a1_ref).astype(BF16), a2_ref))
    gate_ref[0] = dot(_sigmoid(dot(xg, g1_ref)).astype(BF16), g2_ref)

    D = hn.shape[-1]
    seg, seg_t = _head_indicator(D, RWKV_N)
    kkk = k * kk_ref[0]
    rs = lax.rsqrt(_exact_dot(kkk * kkk, seg) + L2_EPS)
    kkn = kkk * _exact_dot(rs, seg_t)
    kf = k * (1.0 + (a - 1.0) * ka_ref[0])
    bonus_ref[0] = _exact_dot(r * kf * rk_ref[0], seg)

    row = t * tm + lax.broadcasted_iota(jnp.int32, (tm, 1), 0)
    m = (row < valid_rows).astype(F32)
    kkn = kkn * m
    r_ref[0] = r
    lw_ref[0] = -jnp.exp(w_log) * m
    kf_ref[0] = kf * m
    v_ref[0] = v * m
    kkn_ref[0] = kkn
    b_ref[0] = kkn * a


def rwkv_pre(x, shift_state, v_first, p, i, layer, lay):
    G, R, D = x.shape
    s = lay['shift']
    tm = _pick_tile(R, RWKV_ROW_TILES)
    pad = -(-s // 8) * 8
    has_vres = layer > 0
    L = lambda *blk: _layer_spec(layer, *blk)
    lora = lambda name: p[name].shape[-1]
    in_specs = [_tile_spec(tm, D), _layer_spec(i, 1, D), L(6, D), L(D, D), L(D, D), L(D, D),
                L(D, lora('rwkv_w1')), L(lora('rwkv_w1'), D), L(D, lora('rwkv_a1')), L(lora('rwkv_a1'), D),
                L(D, lora('rwkv_g1')), L(lora('rwkv_g1'), D), L(1, D), L(1, D), L(1, D), L(1, D), L(1, D),
                _group_spec(s, D)]
    args = [x, p['norm_mix3'], p['rwkv_mix'], p['rwkv_wr'], p['rwkv_wk'], p['rwkv_wv'], p['rwkv_w1'],
            p['rwkv_w2'], p['rwkv_a1'], p['rwkv_a2'], p['rwkv_g1'], p['rwkv_g2'], p['rwkv_w03'], p['rwkv_a03'],
            p['rwkv_k_k3'], p['rwkv_k_a3'], p['rwkv_r_k3'], shift_state]
    if has_vres:
        V = lambda *blk: _layer_spec(layer - 1, *blk)
        in_specs += [V(D, lora('rwkv_v1')), V(lora('rwkv_v1'), D), V(1, D), _tile_spec(tm, D)]
        args += [p['rwkv_v1'], p['rwkv_v2'], p['rwkv_v03'], v_first]
    act = jax.ShapeDtypeStruct((G, R, D), F32)
    kern = functools.partial(_rwkv_pre_kernel, tm=tm, shift=s, state_row=_state_row(R, tm, lay['valid'], s),
                             valid_rows=lay['valid'], has_vres=has_vres)
    return _row_grid_call(
        kern, "rwkv_pre", G, R, tm, in_specs=in_specs,
        out_specs=[_tile_spec(tm, D)] * 7 + [_tile_spec(tm, 128), _group_spec(s, D)],
        out_shape=[act] * 7 + [jax.ShapeDtypeStruct((G, R, 128), F32), jax.ShapeDtypeStruct((G, s, D), F32)],
        scratch_shapes=[pltpu.VMEM((pad + tm, D), F32), pltpu.VMEM((s, D), F32)],
    )(*args)


def _rwkv_post_kernel(x_ref, y_ref, v_ref, gate_ref, bonus_ref, lnw_ref, lnb_ref, wo_ref, o_ref):
    y = y_ref[0]
    D = y.shape[-1]
    seg, seg_t = _head_indicator(D, RWKV_N)
    inv_n = 1.0 / RWKV_N
    mu = _exact_dot(_exact_dot(y, seg) * inv_n, seg_t)
    d = y - mu
    rs = lax.rsqrt(_exact_dot(d * d, seg) * inv_n + GN_EPS)
    yn = d * _exact_dot(rs, seg_t) * lnw_ref[0] + lnb_ref[0]
    yn = yn + _exact_dot(bonus_ref[0], seg_t) * v_ref[0]
    o_ref[0] = x_ref[0] + jnp.dot((yn * gate_ref[0]).astype(BF16), wo_ref[0], preferred_element_type=F32)


def rwkv_post(x, y, v, gate, bonus, p, layer):
    G, R, D = x.shape
    tm = _pick_tile(R, RWKV_ROW_TILES)
    L = lambda *blk: _layer_spec(layer, *blk)
    return _row_grid_call(
        _rwkv_post_kernel, "rwkv_post", G, R, tm,
        in_specs=[_tile_spec(tm, D)] * 4 + [_tile_spec(tm, 128), L(1, D), L(1, D), L(D, D)],
        out_specs=_tile_spec(tm, D),
        out_shape=jax.ShapeDtypeStruct((G, R, D), F32),
        scratch_shapes=[],
    )(x, y, v, gate, bonus, p['rwkv_ln_w3'], p['rwkv_ln_b3'], p['rwkv_wo'])


def _to_seq(a, lay):
    if lay['shift'] == 1:
        return a
    B = lay['shift']
    T = a.shape[1] // B
    a = a.reshape(T, B, a.shape[-1]).transpose(1, 0, 2)
    return jnp.pad(a, ((0, 0), (0, lay['chunk'] - T), (0, 0)))


def _from_seq(o, lay):
    if lay['shift'] == 1:
        return o
    B = lay['shift']
    T = lay['valid'] // B
    return o[:, :T].transpose(1, 0, 2).reshape(1, T * B, o.shape[-1])


def _gdn_mixer(x, conv_buf, S0, earlier, p, i, j, lay):
    qkv_n, z_act, g, beta, new_buf = gdn_pre(x, conv_buf, p, i, j, lay)
    o, S = gdn_recurrence(_to_seq(qkv_n, lay), _to_seq(g, lay), _to_seq(beta, lay), S0, earlier,
                          lay['chunk'], lay['nseq_gdn'])
    return _from_seq(o, lay), z_act, new_buf, S


def _rwkv_mixer(x, shift, S0, earlier, v_first, p, i, j, lay):
    r, lw, kf, v, kk, b, gate, bonus, new_shift = rwkv_pre(x, shift, v_first, p, i, j, lay)
    seq = lambda t: _to_seq(t, lay)
    y, S = rwkv_recurrence(seq(r), seq(lw), seq(kf), seq(v), seq(kk), seq(b), S0, earlier,
                           lay['chunk'], lay['nseq_rwkv'])
    if j == 0:
        v_first = v
    return rwkv_post(x, _from_seq(y, lay), v, gate, bonus, p, j), new_shift, S, v_first


def _trunk(x, gdn_S, gdn_conv, rwkv_S, rwkv_shift, ffn_conv, p, lay):
    depth = p['depth']
    n_gdn, n_rwkv = gdn_S.shape[0], rwkv_S.shape[0]
    gS, gC, rS, rSh, fC = [], [], [], [], []
    v_first = None
    for i in range(depth):
        j = i // 2
        if i % 2 == 0:
            o, z_act, cb, S = _gdn_mixer(x, gdn_conv[j], gdn_S[j], gS if j == n_gdn - 1 else [], p, i, j, lay)
            gS = S if j == n_gdn - 1 else gS + [S[0]]
            gC.append(cb)
            gdn_out = (o, z_act, j)
        else:
            x, sh, S, v_first = _rwkv_mixer(x, rwkv_shift[j], rwkv_S[j], rS if j == n_rwkv - 1 else [], v_first,
                                            p, i, j, lay)
            rS = S if j == n_rwkv - 1 else rS + [S[0]]
            rSh.append(sh)
            gdn_out = None
        x, fb = ffn_layer(x, ffn_conv[i], p, i, lay, final_norm=(i == depth - 1), gdn_out=gdn_out)
        fC.append(fb)
    return x, gS, jnp.stack(gC), rS, jnp.stack(rSh), jnp.stack(fC)


def _pad_tokens(x, chunk):
    L = x.shape[1]
    Lp = -(-L // chunk) * chunk
    return jnp.pad(x, ((0, 0), (0, Lp - L), (0, 0)))


def _rows_state(s):
    N, B, W, C = s.shape
    return s.transpose(0, 2, 1, 3).reshape(N, 1, W * B, C)


def _seq_state(s, B):
    N, _, WB, C = s.shape
    return s.reshape(N, WB // B, B, C).transpose(0, 2, 1, 3)


def kernel(x_prompt, x_sample, state_gdn_S, state_gdn_conv, state_rwkv_S, state_rwkv_shift, state_ffn_conv, meta_tokens, norm_mix, norm_ffn, norm_final, gdn_w_in, gdn_conv_w, gdn_A_log, gdn_dt_bias, gdn_norm_w, gdn_w_out, rwkv_mix, rwkv_wr, rwkv_wk, rwkv_wv, rwkv_wo, rwkv_w0, rwkv_w1, rwkv_w2, rwkv_a0, rwkv_a1, rwkv_a2, rwkv_g1, rwkv_g2, rwkv_k_k, rwkv_k_a, rwkv_r_k, rwkv_ln_w, rwkv_ln_b, rwkv_v0, rwkv_v1, rwkv_v2, ffn_w_up, ffn_conv_w, ffn_w_down):
    bf = lambda w: w.astype(BF16)
    three = lambda a: a.reshape(a.shape[0], 1, a.shape[-1])
    p = dict(depth=norm_mix.shape[0], norm_mix3=three(norm_mix), norm_ffn3=three(norm_ffn), norm_final=norm_final,
             gdn_w_in=bf(gdn_w_in), gdn_conv_w=gdn_conv_w,
             gdn_A_log3=three(gdn_A_log), gdn_dt_bias3=three(gdn_dt_bias), gdn_norm_w3=three(gdn_norm_w),
             gdn_w_out=bf(gdn_w_out), rwkv_mix=rwkv_mix, rwkv_wr=bf(rwkv_wr), rwkv_wk=bf(rwkv_wk),
             rwkv_wv=bf(rwkv_wv), rwkv_wo=bf(rwkv_wo), rwkv_w1=bf(rwkv_w1), rwkv_w2=bf(rwkv_w2),
             rwkv_a1=bf(rwkv_a1), rwkv_a2=bf(rwkv_a2), rwkv_g1=bf(rwkv_g1), rwkv_g2=bf(rwkv_g2),
             rwkv_v1=bf(rwkv_v1), rwkv_v2=bf(rwkv_v2), rwkv_w03=three(rwkv_w0), rwkv_a03=three(rwkv_a0),
             rwkv_v03=three(rwkv_v0), rwkv_k_k3=three(rwkv_k_k), rwkv_k_a3=three(rwkv_k_a),
             rwkv_r_k3=rwkv_r_k.reshape(rwkv_r_k.shape[0], 1, -1), rwkv_ln_w3=three(rwkv_ln_w),
             rwkv_ln_b3=three(rwkv_ln_b), ffn_w_up=bf(ffn_w_up), ffn_conv_w=ffn_conv_w, ffn_w_down=bf(ffn_w_down))

    B = x_prompt.shape[0]
    D = x_prompt.shape[-1]
    meta = jnp.broadcast_to(meta_tokens[None], (B, N_META, D))
    xp = jnp.concatenate([meta, x_prompt], axis=1)
    Lp = xp.shape[1]
    zeros_like_batch = lambda s: jnp.zeros((s.shape[0], B) + s.shape[2:], F32)
    yp, gS_p, gC_p, rS_p, rSh_p, fC_p = _trunk(
        _pad_tokens(xp, PROMPT_CHUNK), zeros_like_batch(state_gdn_S), zeros_like_batch(state_gdn_conv),
        zeros_like_batch(state_rwkv_S), zeros_like_batch(state_rwkv_shift)[:, :, None],
        zeros_like_batch(state_ffn_conv), p, dict(shift=1, valid=Lp, chunk=PROMPT_CHUNK, nseq_gdn=_pick_tile(B, (PROMPT_GDN_SEQS_PER_STEP, 2, 1)),
             nseq_rwkv=_pick_tile(B, (PROMPT_RWKV_SEQS_PER_STEP, 1))))
    y_prompt = yp[:, N_META:Lp]
    rSh_p = rSh_p[:, :, 0]

    Bs, Ls, _ = x_sample.shape
    xs = x_sample.transpose(1, 0, 2).reshape(1, Ls * Bs, D)
    ys, gS_s, gC_s, rS_s, rSh_s, fC_s = _trunk(
        xs, state_gdn_S, _rows_state(state_gdn_conv), state_rwkv_S, state_rwkv_shift[:, None],
        _rows_state(state_ffn_conv), p, dict(shift=Bs, valid=Ls * Bs, chunk=SAMPLE_CHUNK, nseq_gdn=_pick_tile(Bs, (SAMPLE_SEQS_PER_STEP, 4, 2, 1)),
             nseq_rwkv=_pick_tile(Bs, (SAMPLE_SEQS_PER_STEP, 4, 2, 1))))
    y_sample = ys.reshape(Ls, Bs, D).transpose(1, 0, 2)
    gC_s = _seq_state(gC_s, Bs)
    fC_s = _seq_state(fC_s, Bs)
    rSh_s = rSh_s[:, 0]
    return (y_prompt, y_sample, gS_p, gC_p, rS_p, rSh_p, fC_p, gS_s, gC_s, rS_s, rSh_s, fC_s)
```

```python
import functools

import jax
import jax.numpy as jnp
from jax import lax
from jax.experimental import pallas as pl
from jax.experimental.pallas import tpu as pltpu

F32 = jnp.float32
BF16 = jnp.bfloat16

N_META = 16
GDN_HEADS = 8
GDN_DK = 128
GDN_DV = 128
RWKV_N = 64
RMS_EPS = 1e-6
L2_EPS = 1e-6
GN_EPS = 64e-5

PROMPT_CHUNK = 64
SAMPLE_CHUNK = 8
SAMPLE_SEQS_PER_STEP = 8
PROMPT_GDN_SEQS_PER_STEP = 4
PROMPT_RWKV_SEQS_PER_STEP = 2
SOLVE_SPLIT_SPAN = 8
VMEM_LIMIT_BYTES = 48 * 1024 * 1024


def _pick_tile(n, cands):
    for c in cands:
        if n % c == 0:
            return c
    return n


def _dot(a, b):
    return jnp.dot(a.astype(BF16), b.astype(BF16), preferred_element_type=F32)


def _dot_nt(a, b):
    return lax.dot_general(a.astype(BF16), b.astype(BF16), (((1,), (1,)), ((), ())),
                           preferred_element_type=F32)


def _dot_tn(a, b):
    return lax.dot_general(a.astype(BF16), b.astype(BF16), (((0,), (0,)), ((), ())),
                           preferred_element_type=F32)


def _split2(x):
    hi = x.astype(BF16)
    return hi, (x - hi.astype(F32)).astype(BF16)


def _dot3(a, b):
    ah, al = _split2(a)
    bh, bl = _split2(b)
    d = lambda x, y: jnp.dot(x, y, preferred_element_type=F32)
    return d(ah, bh) + (d(ah, bl) + d(al, bh))


LANES = 128
HALF_LANES = 64


def _stacked_state_spec(layers, nseq, state_shape):
    return pl.BlockSpec((layers, nseq) + tuple(state_shape), lambda b, c: (0, b) + (0,) * len(state_shape))


def _write_stacked_states(sout_ref, earlier_refs, s_scr):
    for i, ref in enumerate(earlier_refs):
        sout_ref[i] = ref[...]
    sout_ref[len(earlier_refs)] = s_scr[...]


def _pad_lanes(a):
    return jnp.concatenate([a, jnp.zeros(a.shape[:-1] + (LANES - a.shape[-1],), a.dtype)], axis=-1)


def _pad_lanes_left(a):
    return jnp.concatenate([jnp.zeros(a.shape[:-1] + (LANES - a.shape[-1],), a.dtype), a], axis=-1)


def _cumsum_rows(incl_bf16, x):
    hi = x.astype(BF16)
    r1 = x - hi.astype(F32)
    mid = r1.astype(BF16)
    lo = (r1 - mid.astype(F32)).astype(BF16)
    dot = lambda p: jnp.dot(incl_bf16, p, preferred_element_type=F32)
    return dot(hi) + dot(mid) + dot(lo)


def _gdn_kernel(q_ref, k_ref, v_ref, gcol_ref, bcol_ref, grow_ref, s0_ref, *rest, chunk, heads, nseq):
    earlier_refs, (o_ref, sout_ref, s_scr) = rest[:-3], rest[-3:]
    c = pl.program_id(1)

    @pl.when(c == 0)
    def _():
        s_scr[...] = s0_ref[0]

    ri = lax.broadcasted_iota(jnp.int32, (chunk, chunk), 0)
    ci = lax.broadcasted_iota(jnp.int32, (chunk, chunk), 1)
    causal = ri >= ci
    strict = ri > ci
    units = [(s, h) for s in range(nseq) for h in range(heads)]
    us = range(len(units))
    sl = [slice(h * GDN_DK, (h + 1) * GDN_DK) for _, h in units]
    q = [q_ref[s, :, sl[u]] for u, (s, h) in enumerate(units)]
    k = [k_ref[s, :, sl[u]] for u, (s, h) in enumerate(units)]
    v = [v_ref[s, :, sl[u]] for u, (s, h) in enumerate(units)]
    g_col = [gcol_ref[s, :, h:h + 1] for s, h in units]
    beta = [bcol_ref[s, :, h:h + 1] for s, h in units]
    g_row = [grow_ref[s, 0, h:h + 1, :] for s, h in units]
    G_col = [jnp.sum(jnp.where(causal, g_row[u], 0.0), axis=1, keepdims=True) for u in us]
    G_row = [jnp.sum(jnp.where(ri <= ci, g_col[u], 0.0), axis=0, keepdims=True) for u in us]
    g_last = [jnp.sum(g_row[u], axis=1, keepdims=True) for u in us]
    decay = [jnp.where(causal, jnp.exp(jnp.minimum(G_col[u] - G_row[u], 0.0)), 0.0) for u in us]
    eG = [jnp.exp(G_col[u]) for u in us]
    k_bf = [k[u].astype(BF16) for u in us]
    kb = [k[u] * beta[u] for u in us]
    S = [s_scr[s, h] for s, h in units]
    S_bf = [S[u].astype(BF16) for u in us]
    n_strict = [jnp.where(strict, _dot_nt(kb[u], k_bf[u]) * decay[u], 0.0) for u in us]
    qk = [(_dot_nt(q[u], k_bf[u]) * decay[u]).astype(BF16) for u in us]
    qS = [_dot(q[u] * eG[u], S_bf[u]) for u in us]
    lane = lax.broadcasted_iota(jnp.int32, (chunk, LANES), 1)
    row = lax.broadcasted_iota(jnp.int32, (chunk, LANES), 0)
    eye_hi = (lane == row + HALF_LANES).astype(F32)
    is_p = lane < HALF_LANES
    Z = [_pad_lanes(n_strict[u]) + eye_hi for u in us]
    span = 1
    sign = -1.0
    while span < chunk:
        dot = _dot3 if span < SOLVE_SPLIT_SPAN else _dot
        PZ = [dot(Z[u][:, :chunk], Z[u]) for u in us]
        Z = [jnp.where(is_p, PZ[u], Z[u] + sign * PZ[u]) for u in us]
        sign = 1.0
        span *= 2
    rhs = [jnp.concatenate([v[u] * beta[u], kb[u] * eG[u]], axis=1) for u in us]
    x = [_dot3(Z[u][:, HALF_LANES:HALF_LANES + chunk], rhs[u]) for u in us]
    v_new = [x[u][:, :GDN_DV] - _dot(x[u][:, GDN_DV:], S_bf[u]) for u in us]
    v_new_bf = [v_new[u].astype(BF16) for u in us]
    for u, (s, h) in enumerate(units):
        o_ref[s, :, sl[u]] = qS[u] + _dot(qk[u], v_new_bf[u])
    for u, (s, h) in enumerate(units):
        k_dec = k[u] * jnp.exp(g_last[u] - G_col[u])
        s_scr[s, h] = S[u] * jnp.exp(g_last[u]) + _dot_tn(k_dec, v_new_bf[u])

    @pl.when(c == pl.num_programs(1) - 1)
    def _():
        _write_stacked_states(sout_ref, earlier_refs, s_scr)


def gdn_recurrence(qkv, g, beta, S_all, layer, earlier, chunk, nseq):
    B, L, _ = qkv.shape
    H = GDN_HEADS
    n = L // chunk
    g_row = jnp.transpose(g, (0, 2, 1)).reshape(B, H, n, chunk).transpose(0, 2, 1, 3)
    blk = lambda part: pl.BlockSpec((nseq, chunk, H * GDN_DK), lambda b, c, part=part: (b, c, part))
    col = pl.BlockSpec((nseq, chunk, H), lambda b, c: (b, c, 0))
    st = pl.BlockSpec((nseq, H, GDN_DK, GDN_DV), lambda b, c: (b, 0, 0, 0))
    st0 = pl.BlockSpec((1, nseq, H, GDN_DK, GDN_DV), lambda b, c: (layer, b, 0, 0, 0))
    state_shape = S_all.shape[1:]
    o, S = pl.pallas_call(
        functools.partial(_gdn_kernel, chunk=chunk, heads=H, nseq=nseq),
        grid=(B // nseq, n),
        in_specs=[blk(0), blk(1), blk(2), col, col,
                  pl.BlockSpec((nseq, 1, H, chunk), lambda b, c: (b, c, 0, 0)), st0] + [st] * len(earlier),
        out_specs=[blk(0), _stacked_state_spec(len(earlier) + 1, nseq, state_shape[1:])],
        out_shape=[jax.ShapeDtypeStruct((B, L, H * GDN_DV), F32),
                   jax.ShapeDtypeStruct((len(earlier) + 1,) + state_shape, F32)],
        scratch_shapes=[pltpu.VMEM((nseq, H, GDN_DK, GDN_DV), F32)],
        compiler_params=pltpu.CompilerParams(
            dimension_semantics=("parallel", "arbitrary"),
            vmem_limit_bytes=VMEM_LIMIT_BYTES),
        name="gdn_recurrence",
    )(qkv, qkv, qkv, g, beta, g_row, S_all, *earlier)
    return o, S


def _rwkv_kernel(r_ref, lw_ref, k_ref, v_ref, kk_ref, b_ref, s0_ref, *rest, chunk, heads, nseq):
    earlier_refs, (o_ref, sout_ref, s_scr) = rest[:-3], rest[-3:]
    c = pl.program_id(1)

    @pl.when(c == 0)
    def _():
        s_scr[...] = s0_ref[0]

    ri = lax.broadcasted_iota(jnp.int32, (chunk, chunk), 0)
    ci = lax.broadcasted_iota(jnp.int32, (chunk, chunk), 1)
    incl = (ri >= ci).astype(BF16)
    ri2 = lax.broadcasted_iota(jnp.int32, (2 * chunk, 2 * chunk), 0)
    ci2 = lax.broadcasted_iota(jnp.int32, (2 * chunk, 2 * chunk), 1)
    cj = jnp.where(ci2 >= chunk, ci2 - chunk, ci2)
    keep = ((ri2 < chunk) & (ri2 > cj)) | ((ri2 >= chunk) & ((ri2 - chunk) >= cj))

    rows_k, rows_r, cols_k, cols_b, k_dec, b_dec, v_bf, P_last = [], [], [], [], [], [], [], []
    for s in range(nseq):
        lw = lw_ref[s]
        LP = _cumsum_rows(incl, lw)
        LP_last = LP[chunk - 1:chunk, :]
        P_inv = jnp.exp(-LP)
        dec = jnp.exp(LP_last - LP)
        P_last.append(jnp.exp(LP_last))
        kk = kk_ref[s]
        b = b_ref[s]
        k = k_ref[s]
        rows_k.append((kk * jnp.exp(LP - lw)).astype(BF16))
        rows_r.append((r_ref[s] * jnp.exp(LP)).astype(BF16))
        cols_k.append((k * P_inv).astype(BF16))
        cols_b.append((b * P_inv).astype(BF16))
        k_dec.append((k * dec).astype(BF16))
        b_dec.append((b * dec).astype(BF16))
        v_bf.append(v_ref[s].astype(BF16))

    units = [(s, h) for s in range(nseq) for h in range(heads)]
    us = range(len(units))
    sl = [slice(h * RWKV_N, (h + 1) * RWKV_N) for _, h in units]
    rows = [jnp.concatenate([rows_k[s][:, sl[u]], rows_r[s][:, sl[u]]], axis=0) for u, (s, h) in enumerate(units)]
    cols = [jnp.concatenate([cols_k[s][:, sl[u]], cols_b[s][:, sl[u]]], axis=0) for u, (s, h) in enumerate(units)]
    S = [s_scr[s, h] for s, h in units]
    A = [jnp.where(keep, _dot_nt(rows[u], cols[u]), 0.0).astype(BF16) for u in us]
    base = [_dot_nt(rows[u], S[u]) for u in us]
    vh = [v_bf[s][:, sl[u]] for u, (s, h) in enumerate(units)]
    lane = lax.broadcasted_iota(jnp.int32, (chunk, LANES), 1)
    is_p = lane < HALF_LANES
    zero_v = jnp.zeros((chunk, HALF_LANES), BF16)
    Av = [_dot(A[u][:, :chunk], jnp.concatenate([zero_v, vh[u]], axis=1)) for u in us]
    Z = [jnp.where(is_p, _pad_lanes(A[u][:chunk, chunk:]).astype(F32),
                   _pad_lanes_left(base[u][:chunk]) + Av[u][:chunk]) for u in us]
    span = 1
    sign = -1.0
    while span < chunk:
        PZ = [_dot(Z[u][:, :chunk], Z[u]) for u in us]
        Z = [jnp.where(is_p, PZ[u], Z[u] + sign * PZ[u]) for u in us]
        sign = 1.0
        span *= 2
    sol = [Z[u][:, HALF_LANES:].astype(BF16) for u in us]
    for u, (s, h) in enumerate(units):
        o_ref[s, :, sl[u]] = (base[u][chunk:] + Av[u][chunk:, HALF_LANES:]
                              - _dot(A[u][chunk:, chunk:], sol[u]))
    for u, (s, h) in enumerate(units):
        s_scr[s, h] = (S[u] * P_last[s][:, sl[u]] + _dot_tn(vh[u], k_dec[s][:, sl[u]])
                       - _dot_tn(sol[u], b_dec[s][:, sl[u]]))

    @pl.when(c == pl.num_programs(1) - 1)
    def _():
        _write_stacked_states(sout_ref, earlier_refs, s_scr)


def rwkv_recurrence(r, lw, k, v, kk, b, S_all, layer, earlier, chunk, nseq):
    B, L, D = r.shape
    n = L // chunk
    heads = D // RWKV_N
    blk = pl.BlockSpec((nseq, chunk, D), lambda bb, c: (bb, c, 0))
    st = pl.BlockSpec((nseq, heads, RWKV_N, RWKV_N), lambda bb, c: (bb, 0, 0, 0))
    st0 = pl.BlockSpec((1, nseq, heads, RWKV_N, RWKV_N), lambda bb, c: (layer, bb, 0, 0, 0))
    state_shape = S_all.shape[1:]
    o, S = pl.pallas_call(
        functools.partial(_rwkv_kernel, chunk=chunk, heads=heads, nseq=nseq),
        grid=(B // nseq, n),
        in_specs=[blk] * 6 + [st0] + [st] * len(earlier),
        out_specs=[blk, _stacked_state_spec(len(earlier) + 1, nseq, state_shape[1:])],
        out_shape=[jax.ShapeDtypeStruct((B, L, D), F32),
                   jax.ShapeDtypeStruct((len(earlier) + 1,) + state_shape, F32)],
        scratch_shapes=[pltpu.VMEM((nseq, heads, RWKV_N, RWKV_N), F32)],
        compiler_params=pltpu.CompilerParams(
            dimension_semantics=("parallel", "arbitrary"),
            vmem_limit_bytes=VMEM_LIMIT_BYTES),
        name="rwkv_recurrence",
    )(r, lw, k, v, kk, b, S_all, *earlier)
    return o, S


ROW_TILES = (528, 512, 256, 128, 64, 32, 16, 8)
LAYER_VMEM_LIMIT_BYTES = 58 * 1024 * 1024


def _rms(x, eps):
    return x * lax.rsqrt(jnp.mean(x * x, axis=-1, keepdims=True) + eps)


def _softplus(x):
    return jnp.maximum(x, 0.0) + jnp.log1p(jnp.exp(-jnp.abs(x)))


def _sigmoid(x):
    return 0.5 + 0.5 * jnp.tanh(0.5 * x)


def _silu(x):
    hx = 0.5 * x
    return hx + hx * jnp.tanh(hx)


def _exact_dot(x, const_bf16):
    hi, lo = _split2(x)
    dot = lambda p: jnp.dot(p, const_bf16, preferred_element_type=F32)
    return dot(hi) + dot(lo)


def _head_indicator(d, n):
    shift = n.bit_length() - 1
    assert 1 << shift == n
    ch = lax.shift_right_logical(lax.broadcasted_iota(jnp.int32, (d, 128), 0), shift)
    hd = lax.broadcasted_iota(jnp.int32, (d, 128), 1)
    cht = lax.shift_right_logical(lax.broadcasted_iota(jnp.int32, (128, d), 1), shift)
    hdt = lax.broadcasted_iota(jnp.int32, (128, d), 0)
    return (ch == hd).astype(BF16), (cht == hdt).astype(BF16)


def _shifted_rows(buf, carry_rows, cur, shift, taps, tm):
    prev = carry_rows.shape[0]
    pad = buf.shape[0] - tm
    buf[pad - prev:pad, :] = carry_rows
    buf[pad:, :] = cur
    return [buf[pad - j * shift:pad - j * shift + tm, :] for j in range(1, taps + 1)]


def _row_grid_call(kernel_fn, name, G, R, tm, in_specs, out_specs, out_shape, scratch_shapes):
    return pl.pallas_call(
        kernel_fn, grid=(G, R // tm), in_specs=in_specs, out_specs=out_specs, out_shape=out_shape,
        scratch_shapes=scratch_shapes,
        compiler_params=pltpu.CompilerParams(dimension_semantics=("parallel", "arbitrary"),
                                             vmem_limit_bytes=LAYER_VMEM_LIMIT_BYTES),
        name=name)


def _tile_spec(tm, c):
    return pl.BlockSpec((1, tm, c), lambda g, t: (g, t, 0))


def _group_spec(rows, c):
    return pl.BlockSpec((1, rows, c), lambda g, t: (g, 0, 0))


def _layer_spec(layer, *blk):
    return pl.BlockSpec((1,) + blk, lambda g, t: (layer,) + (0,) * len(blk))


def _state_row(R, tm, valid_rows, prev):
    row = valid_rows - prev - (R // tm - 1) * tm
    assert 0 <= row and prev <= tm, "the new conv/shift state must come from the last row tile"
    return row


def _gdn_out(o_ref, z_ref, nw_ref, w_ref, *, heads):
    o = o_ref[0]
    parts = [_rms(o[:, h * GDN_DV:(h + 1) * GDN_DV], RMS_EPS) * nw_ref[0] for h in range(heads)]
    gated = (jnp.concatenate(parts, axis=1) * z_ref[0]).astype(BF16)
    return jnp.dot(gated, w_ref[0], preferred_element_type=F32)


FFN_COL_CHUNK = 256


def _ffn_kernel(x_ref, nw_ref, wup_ref, cw_ref, wdn_ref, cbuf_ref, fnw_ref, *rest,
                tm, shift, d_ff, state_row, final_norm, gdn_heads):
    mixer_refs, (o_ref, cout_ref, gbuf, carry) = rest[:-4], rest[-4:]
    t = pl.program_id(1)
    x = x_ref[0]
    if gdn_heads:
        x = x + _gdn_out(*mixer_refs, heads=gdn_heads)
    hn = (_rms(x, RMS_EPS) * nw_ref[0]).astype(BF16)
    taps = cw_ref.shape[1] - 1
    prev = taps * shift

    @pl.when(t == 0)
    def _():
        carry[...] = cbuf_ref[0]

    n_chunks = d_ff // FFN_COL_CHUNK

    def up(c):
        cols = slice(c * FFN_COL_CHUNK, (c + 1) * FFN_COL_CHUNK)
        ucols = slice(d_ff + c * FFN_COL_CHUNK, d_ff + (c + 1) * FFN_COL_CHUNK)
        return (jnp.dot(hn, wup_ref[0, :, cols], preferred_element_type=F32),
                jnp.dot(hn, wup_ref[0, :, ucols], preferred_element_type=F32))

    acc = jnp.zeros(x.shape, F32)
    nxt = up(0)
    for c in range(n_chunks):
        cols = slice(c * FFN_COL_CHUNK, (c + 1) * FFN_COL_CHUNK)
        g, u = nxt
        if c + 1 < n_chunks:
            nxt = up(c + 1)
        delayed = _shifted_rows(gbuf.at[c % 2], carry[:, cols], g, shift, taps, tm)
        w = cw_ref[0, :, cols]
        gc = g * w[taps:taps + 1]
        for j in range(1, taps + 1):
            gc = gc + delayed[j - 1] * w[taps - j:taps - j + 1]
        act = (_silu(gc) * u).astype(BF16)
        acc = acc + jnp.dot(act, wdn_ref[0, cols, :], preferred_element_type=F32)
        carry[:, cols] = g[tm - prev:, :]
        cout_ref[0, :, cols] = g[state_row:state_row + prev, :]
    y = x + acc
    if final_norm:
        y = _rms(y, RMS_EPS) * fnw_ref[...]
    o_ref[0] = y


def ffn_layer(x, conv_state, p, layer, lay, final_norm, gdn_out=None):
    G, R, D = x.shape
    d_ff = p['ffn_conv_w'].shape[-1]
    taps = p['ffn_conv_w'].shape[1] - 1
    shift = lay['shift']
    prev = taps * shift
    tm = _pick_tile(R, ROW_TILES)
    pad = -(-prev // 8) * 8
    mixer_specs, mixer_args = [], []
    if gdn_out is not None:
        o, z_act, j = gdn_out
        v_dim = o.shape[-1]
        mixer_specs = [_tile_spec(tm, v_dim), _tile_spec(tm, v_dim), _layer_spec(j, 1, GDN_DV),
                       _layer_spec(j, v_dim, D)]
        mixer_args = [o, z_act, p['gdn_norm_w3'], p['gdn_w_out']]
    kern = functools.partial(_ffn_kernel, tm=tm, shift=shift, d_ff=d_ff,
                             state_row=_state_row(R, tm, lay['valid'], prev), final_norm=final_norm,
                             gdn_heads=GDN_HEADS if gdn_out is not None else 0)
    return _row_grid_call(
        kern, "ffn_layer", G, R, tm,
        in_specs=[_tile_spec(tm, D), _layer_spec(layer, 1, D), _layer_spec(layer, D, 2 * d_ff),
                  _layer_spec(layer, taps + 1, d_ff), _layer_spec(layer, d_ff, D), _group_spec(prev, d_ff),
                  pl.BlockSpec((1, D), lambda g, t: (0, 0))] + mixer_specs,
        out_specs=[_tile_spec(tm, D), _group_spec(prev, d_ff)],
        out_shape=[jax.ShapeDtypeStruct((G, R, D), F32), jax.ShapeDtypeStruct((G, prev, d_ff), F32)],
        scratch_shapes=[pltpu.VMEM((2, pad + tm, FFN_COL_CHUNK), F32), pltpu.VMEM((prev, d_ff), F32)],
    )(x, p['norm_ffn3'], p['ffn_w_up'], p['ffn_conv_w'], p['ffn_w_down'], conv_state,
      p['norm_final'].reshape(1, D), *mixer_args)


GDN_COL_CHUNK = 256


def _gdn_pre_kernel(x_ref, nw_ref, w_ref, cw_ref, al_ref, dt_ref, cbuf_ref,
                    qkv_ref, z_ref, g_ref, beta_ref, cout_ref, gbuf, carry,
                    *, tm, shift, conv_dim, state_row, valid_rows, heads):
    t = pl.program_id(1)
    hn = (_rms(x_ref[0], RMS_EPS) * nw_ref[0]).astype(BF16)
    taps = cw_ref.shape[1] - 1
    prev = taps * shift

    @pl.when(t == 0)
    def _():
        carry[...] = cbuf_ref[0]

    qk_dim = heads * GDN_DK
    for c in range(conv_dim // GDN_COL_CHUNK):
        cols = slice(c * GDN_COL_CHUNK, (c + 1) * GDN_COL_CHUNK)
        pre = jnp.dot(hn, w_ref[0, :, cols], preferred_element_type=F32)
        delayed = _shifted_rows(gbuf.at[c % 2], carry[:, cols], pre, shift, taps, tm)
        w = cw_ref[0, :, cols]
        acc = pre * w[taps:taps + 1]
        for j in range(1, taps + 1):
            acc = acc + delayed[j - 1] * w[taps - j:taps - j + 1]
        act = _silu(acc)
        carry[:, cols] = pre[tm - prev:, :]
        cout_ref[0, :, cols] = pre[state_row:state_row + prev, :]
        if c * GDN_COL_CHUNK < 2 * qk_dim:
            scale = GDN_DK ** -0.5 if c * GDN_COL_CHUNK < qk_dim else 1.0
            for h in range(GDN_COL_CHUNK // GDN_DK):
                a_h = act[:, h * GDN_DK:(h + 1) * GDN_DK]
                n_h = a_h * lax.rsqrt(jnp.sum(a_h * a_h, axis=-1, keepdims=True) + L2_EPS)
                lo = c * GDN_COL_CHUNK + h * GDN_DK
                qkv_ref[0, :, lo:lo + GDN_DK] = n_h * scale if scale != 1.0 else n_h
        else:
            qkv_ref[0, :, cols] = act
    z_dim = z_ref.shape[-1]
    for c in range(z_dim // GDN_COL_CHUNK):
        cols = slice(c * GDN_COL_CHUNK, (c + 1) * GDN_COL_CHUNK)
        wcols = slice(conv_dim + c * GDN_COL_CHUNK, conv_dim + (c + 1) * GDN_COL_CHUNK)
        z = jnp.dot(hn, w_ref[0, :, wcols], preferred_element_type=F32)
        z_ref[0, :, cols] = _silu(z)
    ab_lo = conv_dim + z_dim
    ab = jnp.dot(hn, w_ref[0, :, ab_lo:ab_lo + 2 * heads], preferred_element_type=F32)
    row = t * tm + lax.broadcasted_iota(jnp.int32, (tm, 1), 0)
    m = (row < valid_rows).astype(F32)
    g_ref[0] = -jnp.exp(al_ref[0]) * _softplus(ab[:, :heads] + dt_ref[0]) * m
    beta_ref[0] = _sigmoid(ab[:, heads:2 * heads]) * m


def gdn_pre(x, conv_state, p, i, layer, lay):
    G, R, D = x.shape
    H = GDN_HEADS
    conv_dim = p['gdn_conv_w'].shape[-1]
    w_cols = p['gdn_w_in'].shape[-1]
    z_dim = w_cols - conv_dim - 2 * H
    taps = p['gdn_conv_w'].shape[1] - 1
    shift = lay['shift']
    prev = taps * shift
    tm = _pick_tile(R, ROW_TILES)
    pad = -(-prev // 8) * 8
    kern = functools.partial(_gdn_pre_kernel, tm=tm, shift=shift, conv_dim=conv_dim,
                             state_row=_state_row(R, tm, lay['valid'], prev), valid_rows=lay['valid'], heads=H)
    return _row_grid_call(
        kern, "gdn_pre", G, R, tm,
        in_specs=[_tile_spec(tm, D), _layer_spec(i, 1, D), _layer_spec(layer, D, w_cols),
                  _layer_spec(layer, taps + 1, conv_dim),
                  _layer_spec(layer, 1, H), _layer_spec(layer, 1, H), _group_spec(prev, conv_dim)],
        out_specs=[_tile_spec(tm, conv_dim), _tile_spec(tm, z_dim), _tile_spec(tm, H), _tile_spec(tm, H),
                   _group_spec(prev, conv_dim)],
        out_shape=[jax.ShapeDtypeStruct((G, R, conv_dim), F32), jax.ShapeDtypeStruct((G, R, z_dim), F32),
                   jax.ShapeDtypeStruct((G, R, H), F32), jax.ShapeDtypeStruct((G, R, H), F32),
                   jax.ShapeDtypeStruct((G, prev, conv_dim), F32)],
        scratch_shapes=[pltpu.VMEM((2, pad + tm, GDN_COL_CHUNK), F32), pltpu.VMEM((prev, conv_dim), F32)],
    )(x, p['norm_mix3'], p['gdn_w_in'], p['gdn_conv_w'], p['gdn_A_log3'], p['gdn_dt_bias3'], conv_state)


RWKV_ROW_TILES = (352, 256, 128, 64, 32, 16, 8)


def _rwkv_pre_kernel(*refs, tm, shift, state_row, valid_rows, has_vres):
    (x_ref, nw_ref, mix_ref, wr_ref, wk_ref, wv_ref, w1_ref, w2_ref, a1_ref, a2_ref, g1_ref, g2_ref,
     w0_ref, a0_ref, kk_ref, ka_ref, rk_ref, sh_ref) = refs[:18]
    refs = refs[18:]
    if has_vres:
        v1_ref, v2_ref, v0_ref, vf_ref = refs[:4]
        refs = refs[4:]
    (r_ref, lw_ref, kf_ref, v_ref, kkn_ref, b_ref, gate_ref, bonus_ref, shout_ref, hbuf, carry) = refs
    t = pl.program_id(1)

    @pl.when(t == 0)
    def _():
        carry[...] = sh_ref[0]

    hn = _rms(x_ref[0], RMS_EPS) * nw_ref[0]
    prev = _shifted_rows(hbuf, carry[...], hn, shift, 1, tm)[0]
    carry[...] = hn[tm - shift:, :]
    shout_ref[0] = hn[state_row:state_row + shift, :]
    xx = prev - hn
    mixed = lambda i: (hn + xx * mix_ref[0, i:i + 1, :]).astype(BF16)
    dot = lambda a, w_ref: jnp.dot(a, w_ref[0], preferred_element_type=F32)
    xr, xw, xk, xv, xa, xg = (mixed(i) for i in range(6))
    r = dot(xr, wr_ref)
    k = dot(xk, wk_ref)
    v = dot(xv, wv_ref)
    w_lora = dot(jnp.tanh(dot(xw, w1_ref)).astype(BF16), w2_ref)
    w_log = -_softplus(-(w0_ref[0] + w_lora)) - 0.5
    if has_vres:
        v_lora = dot(dot(xv, v1_ref).astype(BF16), v2_ref)
        v = v + (vf_ref[0] - v) * _sigmoid(v0_ref[0] + v_lora)
    a = _sigmoid(a0_ref[0] + dot(dot(xa, a1_ref).astype(BF16), a2_ref))
    gate_ref[0] = dot(_sigmoid(dot(xg, g1_ref)).astype(BF16), g2_ref)

    D = hn.shape[-1]
    seg, seg_t = _head_indicator(D, RWKV_N)
    kkk = k * kk_ref[0]
    rs = lax.rsqrt(_exact_dot(kkk * kkk, seg) + L2_EPS)
    kkn = kkk * _exact_dot(rs, seg_t)
    kf = k * (1.0 + (a - 1.0) * ka_ref[0])
    bonus_ref[0] = _exact_dot(r * kf * rk_ref[0], seg)

    row = t * tm + lax.broadcasted_iota(jnp.int32, (tm, 1), 0)
    m = (row < valid_rows).astype(F32)
    kkn = kkn * m
    r_ref[0] = r
    lw_ref[0] = -jnp.exp(w_log) * m
    kf_ref[0] = kf * m
    v_ref[0] = v * m
    kkn_ref[0] = kkn
    b_ref[0] = kkn * a


def rwkv_pre(x, shift_state, v_first, p, i, layer, lay):
    G, R, D = x.shape
    s = lay['shift']
    tm = _pick_tile(R, RWKV_ROW_TILES)
    pad = -(-s // 8) * 8
    has_vres = layer > 0
    L = lambda *blk: _layer_spec(layer, *blk)
    lora = lambda name: p[name].shape[-1]
    in_specs = [_tile_spec(tm, D), _layer_spec(i, 1, D), L(6, D), L(D, D), L(D, D), L(D, D),
                L(D, lora('rwkv_w1')), L(lora('rwkv_w1'), D), L(D, lora('rwkv_a1')), L(lora('rwkv_a1'), D),
                L(D, lora('rwkv_g1')), L(lora('rwkv_g1'), D), L(1, D), L(1, D), L(1, D), L(1, D), L(1, D),
                _group_spec(s, D)]
    args = [x, p['norm_mix3'], p['rwkv_mix'], p['rwkv_wr'], p['rwkv_wk'], p['rwkv_wv'], p['rwkv_w1'],
            p['rwkv_w2'], p['rwkv_a1'], p['rwkv_a2'], p['rwkv_g1'], p['rwkv_g2'], p['rwkv_w03'], p['rwkv_a03'],
            p['rwkv_k_k3'], p['rwkv_k_a3'], p['rwkv_r_k3'], shift_state]
    if has_vres:
        V = lambda *blk: _layer_spec(layer - 1, *blk)
        in_specs += [V(D, lora('rwkv_v1')), V(lora('rwkv_v1'), D), V(1, D), _tile_spec(tm, D)]
        args += [p['rwkv_v1'], p['rwkv_v2'], p['rwkv_v03'], v_first]
    act = jax.ShapeDtypeStruct((G, R, D), F32)
    kern = functools.partial(_rwkv_pre_kernel, tm=tm, shift=s, state_row=_state_row(R, tm, lay['valid'], s),
                             valid_rows=lay['valid'], has_vres=has_vres)
    return _row_grid_call(
        kern, "rwkv_pre", G, R, tm, in_specs=in_specs,
        out_specs=[_tile_spec(tm, D)] * 7 + [_tile_spec(tm, 128), _group_spec(s, D)],
        out_shape=[act] * 7 + [jax.ShapeDtypeStruct((G, R, 128), F32), jax.ShapeDtypeStruct((G, s, D), F32)],
        scratch_shapes=[pltpu.VMEM((pad + tm, D), F32), pltpu.VMEM((s, D), F32)],
    )(*args)


def _rwkv_post_kernel(x_ref, y_ref, v_ref, gate_ref, bonus_ref, lnw_ref, lnb_ref, wo_ref, o_ref):
    y = y_ref[0]
    D = y.shape[-1]
    seg, seg_t = _head_indicator(D, RWKV_N)
    inv_n = 1.0 / RWKV_N
    mu = _exact_dot(_exact_dot(y, seg) * inv_n, seg_t)
    d = y - mu
    rs = lax.rsqrt(_exact_dot(d * d, seg) * inv_n + GN_EPS)
    yn = d * _exact_dot(rs, seg_t) * lnw_ref[0] + lnb_ref[0]
    yn = yn + _exact_dot(bonus_ref[0], seg_t) * v_ref[0]
    o_ref[0] = x_ref[0] + jnp.dot((yn * gate_ref[0]).astype(BF16), wo_ref[0], preferred_element_type=F32)


def rwkv_post(x, y, v, gate, bonus, p, layer):
    G, R, D = x.shape
    tm = _pick_tile(R, RWKV_ROW_TILES)
    L = lambda *blk: _layer_spec(layer, *blk)
    return _row_grid_call(
        _rwkv_post_kernel, "rwkv_post", G, R, tm,
        in_specs=[_tile_spec(tm, D)] * 4 + [_tile_spec(tm, 128), L(1, D), L(1, D), L(D, D)],
        out_specs=_tile_spec(tm, D),
        out_shape=jax.ShapeDtypeStruct((G, R, D), F32),
        scratch_shapes=[],
    )(x, y, v, gate, bonus, p['rwkv_ln_w3'], p['rwkv_ln_b3'], p['rwkv_wo'])


def _to_seq(a, lay):
    if lay['shift'] == 1:
        return a
    B = lay['shift']
    T = a.shape[1] // B
    a = a.reshape(T, B, a.shape[-1]).transpose(1, 0, 2)
    return jnp.pad(a, ((0, 0), (0, lay['chunk'] - T), (0, 0)))


def _from_seq(o, lay):
    if lay['shift'] == 1:
        return o
    B = lay['shift']
    T = lay['valid'] // B
    return o[:, :T].transpose(1, 0, 2).reshape(1, T * B, o.shape[-1])


def _gdn_mixer(x, conv_buf, S_all, earlier, p, i, j, lay):
    qkv_n, z_act, g, beta, new_buf = gdn_pre(x, conv_buf, p, i, j, lay)
    o, S = gdn_recurrence(_to_seq(qkv_n, lay), _to_seq(g, lay), _to_seq(beta, lay), S_all, j, earlier,
                          lay['chunk'], lay['nseq_gdn'])
    return _from_seq(o, lay), z_act, new_buf, S


def _rwkv_mixer(x, shift, S_all, earlier, v_first, p, i, j, lay):
    r, lw, kf, v, kk, b, gate, bonus, new_shift = rwkv_pre(x, shift, v_first, p, i, j, lay)
    seq = lambda t: _to_seq(t, lay)
    y, S = rwkv_recurrence(seq(r), seq(lw), seq(kf), seq(v), seq(kk), seq(b), S_all, j, earlier,
                           lay['chunk'], lay['nseq_rwkv'])
    if j == 0:
        v_first = v
    return rwkv_post(x, _from_seq(y, lay), v, gate, bonus, p, j), new_shift, S, v_first


def _trunk(x, gdn_S, gdn_conv, rwkv_S, rwkv_shift, ffn_conv, p, lay):
    depth = p['depth']
    n_gdn, n_rwkv = gdn_S.shape[0], rwkv_S.shape[0]
    gS, gC, rS, rSh, fC = [], [], [], [], []
    v_first = None
    for i in range(depth):
        j = i // 2
        if i % 2 == 0:
            o, z_act, cb, S = _gdn_mixer(x, gdn_conv[j], gdn_S, gS if j == n_gdn - 1 else [], p, i, j, lay)
            gS = S if j == n_gdn - 1 else gS + [S[0]]
            gC.append(cb)
            gdn_out = (o, z_act, j)
        else:
            x, sh, S, v_first = _rwkv_mixer(x, rwkv_shift[j], rwkv_S, rS if j == n_rwkv - 1 else [], v_first,
                                            p, i, j, lay)
            rS = S if j == n_rwkv - 1 else rS + [S[0]]
            rSh.append(sh)
            gdn_out = None
        x, fb = ffn_layer(x, ffn_conv[i], p, i, lay, final_norm=(i == depth - 1), gdn_out=gdn_out)
        fC.append(fb)
    return x, gS, jnp.stack(gC), rS, jnp.stack(rSh), jnp.stack(fC)


def _rows_state(s):
    N, B, W, C = s.shape
    return s.transpose(0, 2, 1, 3).reshape(N, 1, W * B, C)


def _seq_state(s, B):
    N, _, WB, C = s.shape
    return s.reshape(N, WB // B, B, C).transpose(0, 2, 1, 3)


def kernel(x_prompt, x_sample, state_gdn_S, state_gdn_conv, state_rwkv_S, state_rwkv_shift, state_ffn_conv, meta_tokens, norm_mix, norm_ffn, norm_final, gdn_w_in, gdn_conv_w, gdn_A_log, gdn_dt_bias, gdn_norm_w, gdn_w_out, rwkv_mix, rwkv_wr, rwkv_wk, rwkv_wv, rwkv_wo, rwkv_w0, rwkv_w1, rwkv_w2, rwkv_a0, rwkv_a1, rwkv_a2, rwkv_g1, rwkv_g2, rwkv_k_k, rwkv_k_a, rwkv_r_k, rwkv_ln_w, rwkv_ln_b, rwkv_v0, rwkv_v1, rwkv_v2, ffn_w_up, ffn_conv_w, ffn_w_down):
    bf = lambda w: w.astype(BF16)
    three = lambda a: a.reshape(a.shape[0], 1, a.shape[-1])
    p = dict(depth=norm_mix.shape[0], norm_mix3=three(norm_mix), norm_ffn3=three(norm_ffn), norm_final=norm_final,
             gdn_w_in=bf(gdn_w_in), gdn_conv_w=gdn_conv_w,
             gdn_A_log3=three(gdn_A_log), gdn_dt_bias3=three(gdn_dt_bias), gdn_norm_w3=three(gdn_norm_w),
             gdn_w_out=bf(gdn_w_out), rwkv_mix=rwkv_mix, rwkv_wr=bf(rwkv_wr), rwkv_wk=bf(rwkv_wk),
             rwkv_wv=bf(rwkv_wv), rwkv_wo=bf(rwkv_wo), rwkv_w1=bf(rwkv_w1), rwkv_w2=bf(rwkv_w2),
             rwkv_a1=bf(rwkv_a1), rwkv_a2=bf(rwkv_a2), rwkv_g1=bf(rwkv_g1), rwkv_g2=bf(rwkv_g2),
             rwkv_v1=bf(rwkv_v1), rwkv_v2=bf(rwkv_v2), rwkv_w03=three(rwkv_w0), rwkv_a03=three(rwkv_a0),
             rwkv_v03=three(rwkv_v0), rwkv_k_k3=three(rwkv_k_k), rwkv_k_a3=three(rwkv_k_a),
             rwkv_r_k3=rwkv_r_k.reshape(rwkv_r_k.shape[0], 1, -1), rwkv_ln_w3=three(rwkv_ln_w),
             rwkv_ln_b3=three(rwkv_ln_b), ffn_w_up=bf(ffn_w_up), ffn_conv_w=ffn_conv_w, ffn_w_down=bf(ffn_w_down))

    B = x_prompt.shape[0]
    D = x_prompt.shape[-1]
    Lp = N_META + x_prompt.shape[1]
    L_pad = -(-Lp // PROMPT_CHUNK) * PROMPT_CHUNK
    xp = jnp.pad(x_prompt, ((0, 0), (N_META, L_pad - Lp), (0, 0)))
    xp = lax.dynamic_update_slice(xp, jnp.broadcast_to(meta_tokens[None], (B, N_META, D)), (0, 0, 0))
    zeros_like_batch = lambda s: jnp.zeros((s.shape[0], B) + s.shape[2:], F32)
    yp, gS_p, gC_p, rS_p, rSh_p, fC_p = _trunk(
        xp, zeros_like_batch(state_gdn_S), zeros_like_batch(state_gdn_conv),
        zeros_like_batch(state_rwkv_S), zeros_like_batch(state_rwkv_shift)[:, :, None],
        zeros_like_batch(state_ffn_conv), p, dict(shift=1, valid=Lp, chunk=PROMPT_CHUNK, nseq_gdn=_pick_tile(B, (PROMPT_GDN_SEQS_PER_STEP, 2, 1)),
             nseq_rwkv=_pick_tile(B, (PROMPT_RWKV_SEQS_PER_STEP, 1))))
    y_prompt = yp[:, N_META:Lp]
    rSh_p = rSh_p[:, :, 0]

    Bs, Ls, _ = x_sample.shape
    xs = x_sample.transpose(1, 0, 2).reshape(1, Ls * Bs, D)
    ys, gS_s, gC_s, rS_s, rSh_s, fC_s = _trunk(
        xs, state_gdn_S, _rows_state(state_gdn_conv), state_rwkv_S, state_rwkv_shift[:, None],
        _rows_state(state_ffn_conv), p, dict(shift=Bs, valid=Ls * Bs, chunk=SAMPLE_CHUNK, nseq_gdn=_pick_tile(Bs, (SAMPLE_SEQS_PER_STEP, 4, 2, 1)),
             nseq_rwkv=_pick_tile(Bs, (SAMPLE_SEQS_PER_STEP, 4, 2, 1))))
    y_sample = ys.reshape(Ls, Bs, D).transpose(1, 0, 2)
    gC_s = _seq_state(gC_s, Bs)
    fC_s = _seq_state(fC_s, Bs)
    rSh_s = rSh_s[:, 0]
    return (y_prompt, y_sample, gS_p, gC_p, rS_p, rSh_p, fC_p, gS_s, gC_s, rS_s, rSh_s, fC_s)
```

```python
import functools

import jax
import jax.numpy as jnp
from jax import lax
from jax.experimental import pallas as pl
from jax.experimental.pallas import tpu as pltpu

F32 = jnp.float32
BF16 = jnp.bfloat16

N_META = 16
GDN_HEADS = 8
GDN_DK = 128
GDN_DV = 128
RWKV_N = 64
RMS_EPS = 1e-6
L2_EPS = 1e-6
GN_EPS = 64e-5

PROMPT_CHUNK = 64
SAMPLE_CHUNK = 8
SAMPLE_SEQS_PER_STEP = 8
PROMPT_GDN_SEQS_PER_STEP = 4
PROMPT_RWKV_SEQS_PER_STEP = 2
SOLVE_SPLIT_SPAN = 8
VMEM_LIMIT_BYTES = 48 * 1024 * 1024


def _pick_tile(n, cands):
    for c in cands:
        if n % c == 0:
            return c
    return n


def _dot(a, b):
    return jnp.dot(a.astype(BF16), b.astype(BF16), preferred_element_type=F32)


def _dot_nt(a, b):
    return lax.dot_general(a.astype(BF16), b.astype(BF16), (((1,), (1,)), ((), ())),
                           preferred_element_type=F32)


def _dot_tn(a, b):
    return lax.dot_general(a.astype(BF16), b.astype(BF16), (((0,), (0,)), ((), ())),
                           preferred_element_type=F32)


def _split2(x):
    hi = x.astype(BF16)
    return hi, (x - hi.astype(F32)).astype(BF16)


def _dot3(a, b):
    ah, al = _split2(a)
    bh, bl = _split2(b)
    d = lambda x, y: jnp.dot(x, y, preferred_element_type=F32)
    return d(ah, bh) + (d(ah, bl) + d(al, bh))


LANES = 128
HALF_LANES = 64


def _stacked_state_spec(layers, nseq, state_shape):
    return pl.BlockSpec((layers, nseq) + tuple(state_shape), lambda b, c: (0, b) + (0,) * len(state_shape))


def _write_stacked_states(sout_ref, earlier_refs, s_scr):
    for i, ref in enumerate(earlier_refs):
        sout_ref[i] = ref[...]
    sout_ref[len(earlier_refs)] = s_scr[...]


def _pad_lanes(a):
    return jnp.concatenate([a, jnp.zeros(a.shape[:-1] + (LANES - a.shape[-1],), a.dtype)], axis=-1)


def _pad_lanes_left(a):
    return jnp.concatenate([jnp.zeros(a.shape[:-1] + (LANES - a.shape[-1],), a.dtype), a], axis=-1)


def _cumsum_rows(incl_bf16, x):
    hi = x.astype(BF16)
    r1 = x - hi.astype(F32)
    mid = r1.astype(BF16)
    lo = (r1 - mid.astype(F32)).astype(BF16)
    dot = lambda p: jnp.dot(incl_bf16, p, preferred_element_type=F32)
    return dot(hi) + dot(mid) + dot(lo)


def _gdn_kernel(q_ref, k_ref, v_ref, gcol_ref, bcol_ref, grow_ref, s0_ref, *rest, chunk, heads, nseq):
    earlier_refs, (o_ref, sout_ref, s_scr) = rest[:-3], rest[-3:]
    c = pl.program_id(1)

    @pl.when(c == 0)
    def _():
        s_scr[...] = s0_ref[0]

    ri = lax.broadcasted_iota(jnp.int32, (chunk, chunk), 0)
    ci = lax.broadcasted_iota(jnp.int32, (chunk, chunk), 1)
    causal = ri >= ci
    strict = ri > ci
    units = [(s, h) for s in range(nseq) for h in range(heads)]
    us = range(len(units))
    sl = [slice(h * GDN_DK, (h + 1) * GDN_DK) for _, h in units]
    q = [q_ref[s, :, sl[u]] for u, (s, h) in enumerate(units)]
    k = [k_ref[s, :, sl[u]] for u, (s, h) in enumerate(units)]
    v = [v_ref[s, :, sl[u]] for u, (s, h) in enumerate(units)]
    g_col = [gcol_ref[s, :, h:h + 1] for s, h in units]
    beta = [bcol_ref[s, :, h:h + 1] for s, h in units]
    g_row = [grow_ref[s, 0, h:h + 1, :] for s, h in units]
    G_col = [jnp.sum(jnp.where(causal, g_row[u], 0.0), axis=1, keepdims=True) for u in us]
    G_row = [jnp.sum(jnp.where(ri <= ci, g_col[u], 0.0), axis=0, keepdims=True) for u in us]
    g_last = [jnp.sum(g_row[u], axis=1, keepdims=True) for u in us]
    decay = [jnp.where(causal, jnp.exp(jnp.minimum(G_col[u] - G_row[u], 0.0)), 0.0) for u in us]
    eG = [jnp.exp(G_col[u]) for u in us]
    k_bf = [k[u].astype(BF16) for u in us]
    kb = [k[u] * beta[u] for u in us]
    S = [s_scr[s, h] for s, h in units]
    S_bf = [S[u].astype(BF16) for u in us]
    n_strict = [jnp.where(strict, _dot_nt(kb[u], k_bf[u]) * decay[u], 0.0) for u in us]
    qk = [(_dot_nt(q[u], k_bf[u]) * decay[u]).astype(BF16) for u in us]
    qS = [_dot(q[u] * eG[u], S_bf[u]) for u in us]
    lane = lax.broadcasted_iota(jnp.int32, (chunk, LANES), 1)
    row = lax.broadcasted_iota(jnp.int32, (chunk, LANES), 0)
    eye_hi = (lane == row + HALF_LANES).astype(F32)
    is_p = lane < HALF_LANES
    Z = [_pad_lanes(n_strict[u]) + eye_hi for u in us]
    span = 1
    sign = -1.0
    while span < chunk:
        dot = _dot3 if span < SOLVE_SPLIT_SPAN else _dot
        PZ = [dot(Z[u][:, :chunk], Z[u]) for u in us]
        Z = [jnp.where(is_p, PZ[u], Z[u] + sign * PZ[u]) for u in us]
        sign = 1.0
        span *= 2
    rhs = [jnp.concatenate([v[u] * beta[u], kb[u] * eG[u]], axis=1) for u in us]
    x = [_dot3(Z[u][:, HALF_LANES:HALF_LANES + chunk], rhs[u]) for u in us]
    v_new = [x[u][:, :GDN_DV] - _dot(x[u][:, GDN_DV:], S_bf[u]) for u in us]
    v_new_bf = [v_new[u].astype(BF16) for u in us]
    for u, (s, h) in enumerate(units):
        o_ref[s, :, sl[u]] = qS[u] + _dot(qk[u], v_new_bf[u])
    for u, (s, h) in enumerate(units):
        k_dec = k[u] * jnp.exp(g_last[u] - G_col[u])
        s_scr[s, h] = S[u] * jnp.exp(g_last[u]) + _dot_tn(k_dec, v_new_bf[u])

    @pl.when(c == pl.num_programs(1) - 1)
    def _():
        _write_stacked_states(sout_ref, earlier_refs, s_scr)


def gdn_recurrence(qkv, g, beta, S_all, layer, earlier, chunk, nseq):
    B, L, _ = qkv.shape
    H = GDN_HEADS
    n = L // chunk
    g_row = jnp.transpose(g, (0, 2, 1)).reshape(B, H, n, chunk).transpose(0, 2, 1, 3)
    blk = lambda part: pl.BlockSpec((nseq, chunk, H * GDN_DK), lambda b, c, part=part: (b, c, part))
    col = pl.BlockSpec((nseq, chunk, H), lambda b, c: (b, c, 0))
    st = pl.BlockSpec((nseq, H, GDN_DK, GDN_DV), lambda b, c: (b, 0, 0, 0))
    st0 = pl.BlockSpec((1, nseq, H, GDN_DK, GDN_DV), lambda b, c: (layer, b, 0, 0, 0))
    state_shape = S_all.shape[1:]
    o, S = pl.pallas_call(
        functools.partial(_gdn_kernel, chunk=chunk, heads=H, nseq=nseq),
        grid=(B // nseq, n),
        in_specs=[blk(0), blk(1), blk(2), col, col,
                  pl.BlockSpec((nseq, 1, H, chunk), lambda b, c: (b, c, 0, 0)), st0] + [st] * len(earlier),
        out_specs=[blk(0), _stacked_state_spec(len(earlier) + 1, nseq, state_shape[1:])],
        out_shape=[jax.ShapeDtypeStruct((B, L, H * GDN_DV), F32),
                   jax.ShapeDtypeStruct((len(earlier) + 1,) + state_shape, F32)],
        scratch_shapes=[pltpu.VMEM((nseq, H, GDN_DK, GDN_DV), F32)],
        compiler_params=pltpu.CompilerParams(
            dimension_semantics=("parallel", "arbitrary"),
            vmem_limit_bytes=VMEM_LIMIT_BYTES),
        name="gdn_recurrence",
    )(qkv, qkv, qkv, g, beta, g_row, S_all, *earlier)
    return o, S


def _rwkv_kernel(r_ref, lw_ref, k_ref, v_ref, kk_ref, b_ref, s0_ref, *rest, chunk, heads, nseq):
    earlier_refs, (o_ref, sout_ref, s_scr) = rest[:-3], rest[-3:]
    c = pl.program_id(1)

    @pl.when(c == 0)
    def _():
        s_scr[...] = s0_ref[0]

    ri = lax.broadcasted_iota(jnp.int32, (chunk, chunk), 0)
    ci = lax.broadcasted_iota(jnp.int32, (chunk, chunk), 1)
    incl = (ri >= ci).astype(BF16)
    ri2 = lax.broadcasted_iota(jnp.int32, (2 * chunk, 2 * chunk), 0)
    ci2 = lax.broadcasted_iota(jnp.int32, (2 * chunk, 2 * chunk), 1)
    cj = jnp.where(ci2 >= chunk, ci2 - chunk, ci2)
    keep = ((ri2 < chunk) & (ri2 > cj)) | ((ri2 >= chunk) & ((ri2 - chunk) >= cj))

    rows_k, rows_r, cols_k, cols_b, k_dec, b_dec, v_bf, P_last = [], [], [], [], [], [], [], []
    for s in range(nseq):
        lw = lw_ref[s]
        LP = _cumsum_rows(incl, lw)
        LP_last = LP[chunk - 1:chunk, :]
        P_inv = jnp.exp(-LP)
        dec = jnp.exp(LP_last - LP)
        P_last.append(jnp.exp(LP_last))
        kk = kk_ref[s]
        b = b_ref[s]
        k = k_ref[s]
        rows_k.append((kk * jnp.exp(LP - lw)).astype(BF16))
        rows_r.append((r_ref[s] * jnp.exp(LP)).astype(BF16))
        cols_k.append((k * P_inv).astype(BF16))
        cols_b.append((b * P_inv).astype(BF16))
        k_dec.append((k * dec).astype(BF16))
        b_dec.append((b * dec).astype(BF16))
        v_bf.append(v_ref[s].astype(BF16))

    units = [(s, h) for s in range(nseq) for h in range(heads)]
    us = range(len(units))
    sl = [slice(h * RWKV_N, (h + 1) * RWKV_N) for _, h in units]
    rows = [jnp.concatenate([rows_k[s][:, sl[u]], rows_r[s][:, sl[u]]], axis=0) for u, (s, h) in enumerate(units)]
    cols = [jnp.concatenate([cols_k[s][:, sl[u]], cols_b[s][:, sl[u]]], axis=0) for u, (s, h) in enumerate(units)]
    S = [s_scr[s, h] for s, h in units]
    A = [jnp.where(keep, _dot_nt(rows[u], cols[u]), 0.0).astype(BF16) for u in us]
    base = [_dot_nt(rows[u], S[u]) for u in us]
    vh = [v_bf[s][:, sl[u]] for u, (s, h) in enumerate(units)]
    lane = lax.broadcasted_iota(jnp.int32, (chunk, LANES), 1)
    is_p = lane < HALF_LANES
    zero_v = jnp.zeros((chunk, HALF_LANES), BF16)
    Av = [_dot(A[u][:, :chunk], jnp.concatenate([zero_v, vh[u]], axis=1)) for u in us]
    Z = [jnp.where(is_p, _pad_lanes(A[u][:chunk, chunk:]).astype(F32),
                   _pad_lanes_left(base[u][:chunk]) + Av[u][:chunk]) for u in us]
    span = 1
    sign = -1.0
    while span < chunk:
        PZ = [_dot(Z[u][:, :chunk], Z[u]) for u in us]
        Z = [jnp.where(is_p, PZ[u], Z[u] + sign * PZ[u]) for u in us]
        sign = 1.0
        span *= 2
    sol = [Z[u][:, HALF_LANES:].astype(BF16) for u in us]
    for u, (s, h) in enumerate(units):
        o_ref[s, :, sl[u]] = (base[u][chunk:] + Av[u][chunk:, HALF_LANES:]
                              - _dot(A[u][chunk:, chunk:], sol[u]))
    for u, (s, h) in enumerate(units):
        s_scr[s, h] = (S[u] * P_last[s][:, sl[u]] + _dot_tn(vh[u], k_dec[s][:, sl[u]])
                       - _dot_tn(sol[u], b_dec[s][:, sl[u]]))

    @pl.when(c == pl.num_programs(1) - 1)
    def _():
        _write_stacked_states(sout_ref, earlier_refs, s_scr)


def rwkv_recurrence(r, lw, k, v, kk, b, S_all, layer, earlier, chunk, nseq):
    B, L, D = r.shape
    n = L // chunk
    heads = D // RWKV_N
    blk = pl.BlockSpec((nseq, chunk, D), lambda bb, c: (bb, c, 0))
    st = pl.BlockSpec((nseq, heads, RWKV_N, RWKV_N), lambda bb, c: (bb, 0, 0, 0))
    st0 = pl.BlockSpec((1, nseq, heads, RWKV_N, RWKV_N), lambda bb, c: (layer, bb, 0, 0, 0))
    state_shape = S_all.shape[1:]
    o, S = pl.pallas_call(
        functools.partial(_rwkv_kernel, chunk=chunk, heads=heads, nseq=nseq),
        grid=(B // nseq, n),
        in_specs=[blk] * 6 + [st0] + [st] * len(earlier),
        out_specs=[blk, _stacked_state_spec(len(earlier) + 1, nseq, state_shape[1:])],
        out_shape=[jax.ShapeDtypeStruct((B, L, D), F32),
                   jax.ShapeDtypeStruct((len(earlier) + 1,) + state_shape, F32)],
        scratch_shapes=[pltpu.VMEM((nseq, heads, RWKV_N, RWKV_N), F32)],
        compiler_params=pltpu.CompilerParams(
            dimension_semantics=("parallel", "arbitrary"),
            vmem_limit_bytes=VMEM_LIMIT_BYTES),
        name="rwkv_recurrence",
    )(r, lw, k, v, kk, b, S_all, *earlier)
    return o, S


ROW_TILES = (528, 512, 256, 128, 64, 32, 16, 8)
LAYER_VMEM_LIMIT_BYTES = 58 * 1024 * 1024


def _rms(x, eps):
    return x * lax.rsqrt(jnp.mean(x * x, axis=-1, keepdims=True) + eps)


def _softplus(x):
    return jnp.maximum(x, 0.0) + jnp.log1p(jnp.exp(-jnp.abs(x)))


def _sigmoid(x):
    return 0.5 + 0.5 * jnp.tanh(0.5 * x)


def _silu(x):
    hx = 0.5 * x
    return hx + hx * jnp.tanh(hx)


def _exact_dot(x, const_bf16):
    hi, lo = _split2(x)
    dot = lambda p: jnp.dot(p, const_bf16, preferred_element_type=F32)
    return dot(hi) + dot(lo)


def _head_indicator(d, n):
    shift = n.bit_length() - 1
    assert 1 << shift == n
    ch = lax.shift_right_logical(lax.broadcasted_iota(jnp.int32, (d, 128), 0), shift)
    hd = lax.broadcasted_iota(jnp.int32, (d, 128), 1)
    cht = lax.shift_right_logical(lax.broadcasted_iota(jnp.int32, (128, d), 1), shift)
    hdt = lax.broadcasted_iota(jnp.int32, (128, d), 0)
    return (ch == hd).astype(BF16), (cht == hdt).astype(BF16)


def _shifted_rows(buf, carry_rows, cur, shift, taps, tm):
    prev = carry_rows.shape[0]
    pad = buf.shape[0] - tm
    buf[pad - prev:pad, :] = carry_rows
    buf[pad:, :] = cur
    return [buf[pad - j * shift:pad - j * shift + tm, :] for j in range(1, taps + 1)]


def _row_grid_call(kernel_fn, name, G, R, tm, in_specs, out_specs, out_shape, scratch_shapes):
    return pl.pallas_call(
        kernel_fn, grid=(G, R // tm), in_specs=in_specs, out_specs=out_specs, out_shape=out_shape,
        scratch_shapes=scratch_shapes,
        compiler_params=pltpu.CompilerParams(dimension_semantics=("parallel", "arbitrary"),
                                             vmem_limit_bytes=LAYER_VMEM_LIMIT_BYTES),
        name=name)


def _tile_spec(tm, c):
    return pl.BlockSpec((1, tm, c), lambda g, t: (g, t, 0))


def _group_spec(rows, c):
    return pl.BlockSpec((1, rows, c), lambda g, t: (g, 0, 0))


def _layer_spec(layer, *blk):
    return pl.BlockSpec((1,) + blk, lambda g, t: (layer,) + (0,) * len(blk))


def _state_row(R, tm, valid_rows, prev):
    row = valid_rows - prev - (R // tm - 1) * tm
    assert 0 <= row and prev <= tm, "the new conv/shift state must come from the last row tile"
    return row


def _gdn_out(o_ref, z_ref, nw_ref, w_ref, *, heads):
    o = o_ref[0]
    parts = [_rms(o[:, h * GDN_DV:(h + 1) * GDN_DV], RMS_EPS) * nw_ref[0] for h in range(heads)]
    gated = (jnp.concatenate(parts, axis=1) * z_ref[0]).astype(BF16)
    return jnp.dot(gated, w_ref[0], preferred_element_type=F32)


FFN_COL_CHUNK = 256


def _ffn_kernel(x_ref, nw_ref, wup_ref, cw_ref, wdn_ref, cbuf_ref, fnw_ref, *rest,
                tm, shift, d_ff, state_row, final_norm, gdn_heads):
    mixer_refs, (o_ref, cout_ref, gbuf, carry) = rest[:-4], rest[-4:]
    t = pl.program_id(1)
    x = x_ref[0]
    if gdn_heads:
        x = x + _gdn_out(*mixer_refs, heads=gdn_heads)
    hn = (_rms(x, RMS_EPS) * nw_ref[0]).astype(BF16)
    taps = cw_ref.shape[1] - 1
    prev = taps * shift

    @pl.when(t == 0)
    def _():
        carry[...] = cbuf_ref[0]

    n_chunks = d_ff // FFN_COL_CHUNK

    def up(c):
        cols = slice(c * FFN_COL_CHUNK, (c + 1) * FFN_COL_CHUNK)
        ucols = slice(d_ff + c * FFN_COL_CHUNK, d_ff + (c + 1) * FFN_COL_CHUNK)
        return (jnp.dot(hn, wup_ref[0, :, cols], preferred_element_type=F32),
                jnp.dot(hn, wup_ref[0, :, ucols], preferred_element_type=F32))

    acc = jnp.zeros(x.shape, F32)
    nxt = up(0)
    for c in range(n_chunks):
        cols = slice(c * FFN_COL_CHUNK, (c + 1) * FFN_COL_CHUNK)
        g, u = nxt
        if c + 1 < n_chunks:
            nxt = up(c + 1)
        delayed = _shifted_rows(gbuf.at[c % 2], carry[:, cols], g, shift, taps, tm)
        w = cw_ref[0, :, cols]
        gc = g * w[taps:taps + 1]
        for j in range(1, taps + 1):
            gc = gc + delayed[j - 1] * w[taps - j:taps - j + 1]
        act = (_silu(gc) * u).astype(BF16)
        acc = acc + jnp.dot(act, wdn_ref[0, cols, :], preferred_element_type=F32)
        carry[:, cols] = g[tm - prev:, :]
        cout_ref[0, :, cols] = g[state_row:state_row + prev, :]
    y = x + acc
    if final_norm:
        y = _rms(y, RMS_EPS) * fnw_ref[...]
    o_ref[0] = y


def ffn_layer(x, conv_state, p, layer, lay, final_norm, gdn_out=None):
    G, R, D = x.shape
    d_ff = p['ffn_conv_w'].shape[-1]
    taps = p['ffn_conv_w'].shape[1] - 1
    shift = lay['shift']
    prev = taps * shift
    tm = _pick_tile(R, ROW_TILES)
    pad = -(-prev // 8) * 8
    mixer_specs, mixer_args = [], []
    if gdn_out is not None:
        o, z_act, j = gdn_out
        v_dim = o.shape[-1]
        mixer_specs = [_tile_spec(tm, v_dim), _tile_spec(tm, v_dim), _layer_spec(j, 1, GDN_DV),
                       _layer_spec(j, v_dim, D)]
        mixer_args = [o, z_act, p['gdn_norm_w3'], p['gdn_w_out']]
    kern = functools.partial(_ffn_kernel, tm=tm, shift=shift, d_ff=d_ff,
                             state_row=_state_row(R, tm, lay['valid'], prev), final_norm=final_norm,
                             gdn_heads=GDN_HEADS if gdn_out is not None else 0)
    return _row_grid_call(
        kern, "ffn_layer", G, R, tm,
        in_specs=[_tile_spec(tm, D), _layer_spec(layer, 1, D), _layer_spec(layer, D, 2 * d_ff),
                  _layer_spec(layer, taps + 1, d_ff), _layer_spec(layer, d_ff, D), _group_spec(prev, d_ff),
                  pl.BlockSpec((1, D), lambda g, t: (0, 0))] + mixer_specs,
        out_specs=[_tile_spec(tm, D), _group_spec(prev, d_ff)],
        out_shape=[jax.ShapeDtypeStruct((G, R, D), F32), jax.ShapeDtypeStruct((G, prev, d_ff), F32)],
        scratch_shapes=[pltpu.VMEM((2, pad + tm, FFN_COL_CHUNK), F32), pltpu.VMEM((prev, d_ff), F32)],
    )(x, p['norm_ffn3'], p['ffn_w_up'], p['ffn_conv_w'], p['ffn_w_down'], conv_state,
      p['norm_final'].reshape(1, D), *mixer_args)


GDN_COL_CHUNK = 256


def _gdn_pre_kernel(x_ref, nw_ref, w_ref, cw_ref, al_ref, dt_ref, cbuf_ref,
                    qkv_ref, z_ref, g_ref, beta_ref, cout_ref, gbuf, carry,
                    *, tm, shift, conv_dim, state_row, valid_rows, heads):
    t = pl.program_id(1)
    hn = (_rms(x_ref[0], RMS_EPS) * nw_ref[0]).astype(BF16)
    taps = cw_ref.shape[1] - 1
    prev = taps * shift

    @pl.when(t == 0)
    def _():
        carry[...] = cbuf_ref[0]

    qk_dim = heads * GDN_DK
    for c in range(conv_dim // GDN_COL_CHUNK):
        cols = slice(c * GDN_COL_CHUNK, (c + 1) * GDN_COL_CHUNK)
        pre = jnp.dot(hn, w_ref[0, :, cols], preferred_element_type=F32)
        delayed = _shifted_rows(gbuf.at[c % 2], carry[:, cols], pre, shift, taps, tm)
        w = cw_ref[0, :, cols]
        acc = pre * w[taps:taps + 1]
        for j in range(1, taps + 1):
            acc = acc + delayed[j - 1] * w[taps - j:taps - j + 1]
        act = _silu(acc)
        carry[:, cols] = pre[tm - prev:, :]
        cout_ref[0, :, cols] = pre[state_row:state_row + prev, :]
        if c * GDN_COL_CHUNK < 2 * qk_dim:
            scale = GDN_DK ** -0.5 if c * GDN_COL_CHUNK < qk_dim else 1.0
            for h in range(GDN_COL_CHUNK // GDN_DK):
                a_h = act[:, h * GDN_DK:(h + 1) * GDN_DK]
                n_h = a_h * lax.rsqrt(jnp.sum(a_h * a_h, axis=-1, keepdims=True) + L2_EPS)
                lo = c * GDN_COL_CHUNK + h * GDN_DK
                qkv_ref[0, :, lo:lo + GDN_DK] = n_h * scale if scale != 1.0 else n_h
        else:
            qkv_ref[0, :, cols] = act
    z_dim = z_ref.shape[-1]
    for c in range(z_dim // GDN_COL_CHUNK):
        cols = slice(c * GDN_COL_CHUNK, (c + 1) * GDN_COL_CHUNK)
        wcols = slice(conv_dim + c * GDN_COL_CHUNK, conv_dim + (c + 1) * GDN_COL_CHUNK)
        z = jnp.dot(hn, w_ref[0, :, wcols], preferred_element_type=F32)
        z_ref[0, :, cols] = _silu(z)
    ab_lo = conv_dim + z_dim
    ab = jnp.dot(hn, w_ref[0, :, ab_lo:ab_lo + 2 * heads], preferred_element_type=F32)
    row = t * tm + lax.broadcasted_iota(jnp.int32, (tm, 1), 0)
    m = (row < valid_rows).astype(F32)
    g_ref[0] = -jnp.exp(al_ref[0]) * _softplus(ab[:, :heads] + dt_ref[0]) * m
    beta_ref[0] = _sigmoid(ab[:, heads:2 * heads]) * m


def gdn_pre(x, conv_state, p, i, layer, lay):
    G, R, D = x.shape
    H = GDN_HEADS
    conv_dim = p['gdn_conv_w'].shape[-1]
    w_cols = p['gdn_w_in'].shape[-1]
    z_dim = w_cols - conv_dim - 2 * H
    taps = p['gdn_conv_w'].shape[1] - 1
    shift = lay['shift']
    prev = taps * shift
    tm = _pick_tile(R, ROW_TILES)
    pad = -(-prev // 8) * 8
    kern = functools.partial(_gdn_pre_kernel, tm=tm, shift=shift, conv_dim=conv_dim,
                             state_row=_state_row(R, tm, lay['valid'], prev), valid_rows=lay['valid'], heads=H)
    return _row_grid_call(
        kern, "gdn_pre", G, R, tm,
        in_specs=[_tile_spec(tm, D), _layer_spec(i, 1, D), _layer_spec(layer, D, w_cols),
                  _layer_spec(layer, taps + 1, conv_dim),
                  _layer_spec(layer, 1, H), _layer_spec(layer, 1, H), _group_spec(prev, conv_dim)],
        out_specs=[_tile_spec(tm, conv_dim), _tile_spec(tm, z_dim), _tile_spec(tm, H), _tile_spec(tm, H),
                   _group_spec(prev, conv_dim)],
        out_shape=[jax.ShapeDtypeStruct((G, R, conv_dim), F32), jax.ShapeDtypeStruct((G, R, z_dim), F32),
                   jax.ShapeDtypeStruct((G, R, H), F32), jax.ShapeDtypeStruct((G, R, H), F32),
                   jax.ShapeDtypeStruct((G, prev, conv_dim), F32)],
        scratch_shapes=[pltpu.VMEM((2, pad + tm, GDN_COL_CHUNK), F32), pltpu.VMEM((prev, conv_dim), F32)],
    )(x, p['norm_mix3'], p['gdn_w_in'], p['gdn_conv_w'], p['gdn_A_log3'], p['gdn_dt_bias3'], conv_state)


RWKV_ROW_TILES = (352, 256, 128, 64, 32, 16, 8)


def _rwkv_pre_kernel(*refs, tm, shift, state_row, valid_rows, has_vres):
    (x_ref, nw_ref, mix_ref, wr_ref, wk_ref, wv_ref, w1_ref, w2_ref, a1_ref, a2_ref, g1_ref, g2_ref,
     w0_ref, a0_ref, kk_ref, ka_ref, rk_ref, sh_ref) = refs[:18]
    refs = refs[18:]
    if has_vres:
        v1_ref, v2_ref, v0_ref, vf_ref = refs[:4]
        refs = refs[4:]
    (r_ref, lw_ref, kf_ref, v_ref, kkn_ref, b_ref, gate_ref, bonus_ref, shout_ref, hbuf, carry) = refs
    t = pl.program_id(1)

    @pl.when(t == 0)
    def _():
        carry[...] = sh_ref[0]

    hn = _rms(x_ref[0], RMS_EPS) * nw_ref[0]
    prev = _shifted_rows(hbuf, carry[...], hn, shift, 1, tm)[0]
    carry[...] = hn[tm - shift:, :]
    shout_ref[0] = hn[state_row:state_row + shift, :]
    xx = prev - hn
    mixed = lambda i: (hn + xx * mix_ref[0, i:i + 1, :]).astype(BF16)
    dot = lambda a, w_ref: jnp.dot(a, w_ref[0], preferred_element_type=F32)
    xr, xw, xk, xv, xa, xg = (mixed(i) for i in range(6))
    r = dot(xr, wr_ref)
    k = dot(xk, wk_ref)
    v = dot(xv, wv_ref)
    w_lora = dot(jnp.tanh(dot(xw, w1_ref)).astype(BF16), w2_ref)
    w_log = -_softplus(-(w0_ref[0] + w_lora)) - 0.5
    if has_vres:
        v_lora = dot(dot(xv, v1_ref).astype(BF16), v2_ref)
        v = v + (vf_ref[0] - v) * _sigmoid(v0_ref[0] + v_lora)
    a = _sigmoid(a0_ref[0] + dot(dot(xa, a1_ref).astype(BF16), a2_ref))
    gate_ref[0] = dot(_sigmoid(dot(xg, g1_ref)).astype(BF16), g2_ref)

    D = hn.shape[-1]
    seg, seg_t = _head_indicator(D, RWKV_N)
    kkk = k * kk_ref[0]
    rs = lax.rsqrt(_exact_dot(kkk * kkk, seg) + L2_EPS)
    kkn = kkk * _exact_dot(rs, seg_t)
    kf = k * (1.0 + (a - 1.0) * ka_ref[0])
    bonus_ref[0] = _exact_dot(r * kf * rk_ref[0], seg)

    row = t * tm + lax.broadcasted_iota(jnp.int32, (tm, 1), 0)
    m = (row < valid_rows).astype(F32)
    kkn = kkn * m
    r_ref[0] = r
    lw_ref[0] = -jnp.exp(w_log) * m
    kf_ref[0] = kf * m
    v_ref[0] = v * m
    kkn_ref[0] = kkn
    b_ref[0] = kkn * a


def rwkv_pre(x, shift_state, v_first, p, i, layer, lay):
    G, R, D = x.shape
    s = lay['shift']
    tm = _pick_tile(R, RWKV_ROW_TILES)
    pad = -(-s // 8) * 8
    has_vres = layer > 0
    L = lambda *blk: _layer_spec(layer, *blk)
    lora = lambda name: p[name].shape[-1]
    in_specs = [_tile_spec(tm, D), _layer_spec(i, 1, D), L(6, D), L(D, D), L(D, D), L(D, D),
                L(D, lora('rwkv_w1')), L(lora('rwkv_w1'), D), L(D, lora('rwkv_a1')), L(lora('rwkv_a1'), D),
                L(D, lora('rwkv_g1')), L(lora('rwkv_g1'), D), L(1, D), L(1, D), L(1, D), L(1, D), L(1, D),
                _group_spec(s, D)]
    args = [x, p['norm_mix3'], p['rwkv_mix'], p['rwkv_wr'], p['rwkv_wk'], p['rwkv_wv'], p['rwkv_w1'],
            p['rwkv_w2'], p['rwkv_a1'], p['rwkv_a2'], p['rwkv_g1'], p['rwkv_g2'], p['rwkv_w03'], p['rwkv_a03'],
            p['rwkv_k_k3'], p['rwkv_k_a3'], p['rwkv_r_k3'], shift_state]
    if has_vres:
        V = lambda *blk: _layer_spec(layer - 1, *blk)
        in_specs += [V(D, lora('rwkv_v1')), V(lora('rwkv_v1'), D), V(1, D), _tile_spec(tm, D)]
        args += [p['rwkv_v1'], p['rwkv_v2'], p['rwkv_v03'], v_first]
    act = jax.ShapeDtypeStruct((G, R, D), F32)
    kern = functools.partial(_rwkv_pre_kernel, tm=tm, shift=s, state_row=_state_row(R, tm, lay['valid'], s),
                             valid_rows=lay['valid'], has_vres=has_vres)
    return _row_grid_call(
        kern, "rwkv_pre", G, R, tm, in_specs=in_specs,
        out_specs=[_tile_spec(tm, D)] * 7 + [_tile_spec(tm, 128), _group_spec(s, D)],
        out_shape=[act] * 7 + [jax.ShapeDtypeStruct((G, R, 128), F32), jax.ShapeDtypeStruct((G, s, D), F32)],
        scratch_shapes=[pltpu.VMEM((pad + tm, D), F32), pltpu.VMEM((s, D), F32)],
    )(*args)


def _rwkv_post_kernel(x_ref, y_ref, v_ref, gate_ref, bonus_ref, lnw_ref, lnb_ref, wo_ref, o_ref):
    y = y_ref[0]
    D = y.shape[-1]
    seg, seg_t = _head_indicator(D, RWKV_N)
    inv_n = 1.0 / RWKV_N
    mu = _exact_dot(_exact_dot(y, seg) * inv_n, seg_t)
    d = y - mu
    rs = lax.rsqrt(_exact_dot(d * d, seg) * inv_n + GN_EPS)
    yn = d * _exact_dot(rs, seg_t) * lnw_ref[0] + lnb_ref[0]
    yn = yn + _exact_dot(bonus_ref[0], seg_t) * v_ref[0]
    o_ref[0] = x_ref[0] + jnp.dot((yn * gate_ref[0]).astype(BF16), wo_ref[0], preferred_element_type=F32)


def rwkv_post(x, y, v, gate, bonus, p, layer):
    G, R, D = x.shape
    tm = _pick_tile(R, RWKV_ROW_TILES)
    L = lambda *blk: _layer_spec(layer, *blk)
    return _row_grid_call(
        _rwkv_post_kernel, "rwkv_post", G, R, tm,
        in_specs=[_tile_spec(tm, D)] * 4 + [_tile_spec(tm, 128), L(1, D), L(1, D), L(D, D)],
        out_specs=_tile_spec(tm, D),
        out_shape=jax.ShapeDtypeStruct((G, R, D), F32),
        scratch_shapes=[],
    )(x, y, v, gate, bonus, p['rwkv_ln_w3'], p['rwkv_ln_b3'], p['rwkv_wo'])


def _to_seq(a, lay):
    if lay['shift'] == 1:
        return a
    B = lay['shift']
    T = a.shape[1] // B
    a = a.reshape(T, B, a.shape[-1]).transpose(1, 0, 2)
    return jnp.pad(a, ((0, 0), (0, lay['chunk'] - T), (0, 0)))


def _from_seq(o, lay):
    if lay['shift'] == 1:
        return o
    B = lay['shift']
    T = lay['valid'] // B
    return o[:, :T].transpose(1, 0, 2).reshape(1, T * B, o.shape[-1])


def _gdn_mixer(x, conv_buf, S_all, earlier, p, i, j, lay):
    qkv_n, z_act, g, beta, new_buf = gdn_pre(x, conv_buf, p, i, j, lay)
    o, S = gdn_recurrence(_to_seq(qkv_n, lay), _to_seq(g, lay), _to_seq(beta, lay), S_all, j, earlier,
                          lay['chunk'], lay['nseq_gdn'])
    return _from_seq(o, lay), z_act, new_buf, S


def _rwkv_mixer(x, shift, S_all, earlier, v_first, p, i, j, lay):
    r, lw, kf, v, kk, b, gate, bonus, new_shift = rwkv_pre(x, shift, v_first, p, i, j, lay)
    seq = lambda t: _to_seq(t, lay)
    y, S = rwkv_recurrence(seq(r), seq(lw), seq(kf), seq(v), seq(kk), seq(b), S_all, j, earlier,
                           lay['chunk'], lay['nseq_rwkv'])
    if j == 0:
        v_first = v
    return rwkv_post(x, _from_seq(y, lay), v, gate, bonus, p, j), new_shift, S, v_first


def _trunk(x, gdn_S, gdn_conv, rwkv_S, rwkv_shift, ffn_conv, p, lay):
    depth = p['depth']
    n_gdn, n_rwkv = gdn_S.shape[0], rwkv_S.shape[0]
    gS, gC, rS, rSh, fC = [], [], [], [], []
    v_first = None
    for i in range(depth):
        j = i // 2
        if i % 2 == 0:
            o, z_act, cb, S = _gdn_mixer(x, gdn_conv[j], gdn_S, gS if j == n_gdn - 1 else [], p, i, j, lay)
            gS = S if j == n_gdn - 1 else gS + [S[0]]
            gC.append(cb)
            gdn_out = (o, z_act, j)
        else:
            x, sh, S, v_first = _rwkv_mixer(x, rwkv_shift[j], rwkv_S, rS if j == n_rwkv - 1 else [], v_first,
                                            p, i, j, lay)
            rS = S if j == n_rwkv - 1 else rS + [S[0]]
            rSh.append(sh)
            gdn_out = None
        x, fb = ffn_layer(x, ffn_conv[i], p, i, lay, final_norm=(i == depth - 1), gdn_out=gdn_out)
        fC.append(fb)
    return x, gS, jnp.stack(gC), rS, jnp.stack(rSh), jnp.stack(fC)


def _rows_state(s):
    N, B, W, C = s.shape
    return s.transpose(0, 2, 1, 3).reshape(N, 1, W * B, C)


def _seq_state(s, B):
    N, _, WB, C = s.shape
    return s.reshape(N, WB // B, B, C).transpose(0, 2, 1, 3)


def kernel(x_prompt, x_sample, state_gdn_S, state_gdn_conv, state_rwkv_S, state_rwkv_shift, state_ffn_conv, meta_tokens, norm_mix, norm_ffn, norm_final, gdn_w_in, gdn_conv_w, gdn_A_log, gdn_dt_bias, gdn_norm_w, gdn_w_out, rwkv_mix, rwkv_wr, rwkv_wk, rwkv_wv, rwkv_wo, rwkv_w0, rwkv_w1, rwkv_w2, rwkv_a0, rwkv_a1, rwkv_a2, rwkv_g1, rwkv_g2, rwkv_k_k, rwkv_k_a, rwkv_r_k, rwkv_ln_w, rwkv_ln_b, rwkv_v0, rwkv_v1, rwkv_v2, ffn_w_up, ffn_conv_w, ffn_w_down):
    bf = lambda w: w.astype(BF16)
    three = lambda a: a.reshape(a.shape[0], 1, a.shape[-1])
    p = dict(depth=norm_mix.shape[0], norm_mix3=three(norm_mix), norm_ffn3=three(norm_ffn), norm_final=norm_final,
             gdn_w_in=bf(gdn_w_in), gdn_conv_w=gdn_conv_w,
             gdn_A_log3=three(gdn_A_log), gdn_dt_bias3=three(gdn_dt_bias), gdn_norm_w3=three(gdn_norm_w),
             gdn_w_out=bf(gdn_w_out), rwkv_mix=rwkv_mix, rwkv_wr=bf(rwkv_wr), rwkv_wk=bf(rwkv_wk),
             rwkv_wv=bf(rwkv_wv), rwkv_wo=bf(rwkv_wo), rwkv_w1=bf(rwkv_w1), rwkv_w2=bf(rwkv_w2),
             rwkv_a1=bf(rwkv_a1), rwkv_a2=bf(rwkv_a2), rwkv_g1=bf(rwkv_g1), rwkv_g2=bf(rwkv_g2),
             rwkv_v1=bf(rwkv_v1), rwkv_v2=bf(rwkv_v2), rwkv_w03=three(rwkv_w0), rwkv_a03=three(rwkv_a0),
             rwkv_v03=three(rwkv_v0), rwkv_k_k3=three(rwkv_k_k), rwkv_k_a3=three(rwkv_k_a),
             rwkv_r_k3=rwkv_r_k.reshape(rwkv_r_k.shape[0], 1, -1), rwkv_ln_w3=three(rwkv_ln_w),
             rwkv_ln_b3=three(rwkv_ln_b), ffn_w_up=bf(ffn_w_up), ffn_conv_w=ffn_conv_w, ffn_w_down=bf(ffn_w_down))

    B = x_prompt.shape[0]
    D = x_prompt.shape[-1]
    Lp = N_META + x_prompt.shape[1]
    L_pad = -(-Lp // PROMPT_CHUNK) * PROMPT_CHUNK
    xp = jnp.concatenate([jnp.broadcast_to(meta_tokens[None], (B, N_META, D)), x_prompt,
                          jnp.zeros((B, L_pad - Lp, D), F32)], axis=1)
    zeros_like_batch = lambda s: jnp.zeros((s.shape[0], B) + s.shape[2:], F32)
    yp, gS_p, gC_p, rS_p, rSh_p, fC_p = _trunk(
        xp, zeros_like_batch(state_gdn_S), zeros_like_batch(state_gdn_conv),
        zeros_like_batch(state_rwkv_S), zeros_like_batch(state_rwkv_shift)[:, :, None],
        zeros_like_batch(state_ffn_conv), p, dict(shift=1, valid=Lp, chunk=PROMPT_CHUNK, nseq_gdn=_pick_tile(B, (PROMPT_GDN_SEQS_PER_STEP, 2, 1)),
             nseq_rwkv=_pick_tile(B, (PROMPT_RWKV_SEQS_PER_STEP, 1))))
    y_prompt = yp[:, N_META:Lp]
    rSh_p = rSh_p[:, :, 0]

    Bs, Ls, _ = x_sample.shape
    xs = x_sample.transpose(1, 0, 2).reshape(1, Ls * Bs, D)
    ys, gS_s, gC_s, rS_s, rSh_s, fC_s = _trunk(
        xs, state_gdn_S, _rows_state(state_gdn_conv), state_rwkv_S, state_rwkv_shift[:, None],
        _rows_state(state_ffn_conv), p, dict(shift=Bs, valid=Ls * Bs, chunk=SAMPLE_CHUNK, nseq_gdn=_pick_tile(Bs, (SAMPLE_SEQS_PER_STEP, 4, 2, 1)),
             nseq_rwkv=_pick_tile(Bs, (SAMPLE_SEQS_PER_STEP, 4, 2, 1))))
    y_sample = ys.reshape(Ls, Bs, D).transpose(1, 0, 2)
    gC_s = _seq_state(gC_s, Bs)
    fC_s = _seq_state(fC_s, Bs)
    rSh_s = rSh_s[:, 0]
    return (y_prompt, y_sample, gS_p, gC_p, rS_p, rSh_p, fC_p, gS_s, gC_s, rS_s, rSh_s, fC_s)
```
